```python
import jax, jax.numpy as jnp
from jax import lax
import numpy as np

D_MODEL = 1024
BATCH = 4
SEQ = 8192
DEPTH = 4

CHUNK = 64
PLE_DIM = 256
HEAD_DIM = 64
D_POOL = D_MODEL // 4
D_SB = (D_MODEL - D_POOL) // 2
D_LRU = D_MODEL - D_POOL - D_SB
D_MIX = D_SB + D_LRU + D_POOL
SB_HEADS = D_SB // HEAD_DIM
LRU_HEADS = D_LRU // HEAD_DIM
POOL_WINDOWS = (2, 4, 8, 16)
POOL_GROUP = D_POOL // len(POOL_WINDOWS)
D_IN = 3 * D_SB + 2 * D_LRU + D_POOL
CONV_WIDTH = 4
LRU_C = 8.0
D_FF = ((8 * D_MODEL // 3 + 127) // 128) * 128
Q_BLOCK = 128
EPS = 1e-6

kernel_name = 'hybrid_sb_lru_pool_macaron'


def rms_norm(x, g):
    xf = x.astype(jnp.float32)
    y = xf * lax.rsqrt(jnp.mean(xf * xf, axis=-1, keepdims=True) + EPS)
    return (y * g.astype(jnp.float32)).astype(x.dtype)


def swiglu_ffn(x, g, w1, w3, w2):
    h = rms_norm(x, g)
    return (jax.nn.silu(h @ w1) * (h @ w3)) @ w2


def stick_breaking_attention(q, k, v):
    b, s, h, dh = q.shape
    nb = s // Q_BLOCK
    scale = dh ** -0.5
    kt = k.transpose(0, 2, 1, 3)
    vt = v.transpose(0, 2, 1, 3)
    qb = q.reshape(b, nb, Q_BLOCK, h, dh).transpose(1, 0, 3, 2, 4)
    q_pos = jnp.arange(s, dtype=jnp.int32).reshape(nb, Q_BLOCK)
    k_pos = jnp.arange(s, dtype=jnp.int32)

    def block(args):
        q_blk, pos = args
        z = jnp.einsum('bhqd,bhkd->bhqk', q_blk, kt).astype(jnp.float32) * scale
        mask = k_pos[None, :] < pos[:, None]
        log_beta = jax.nn.log_sigmoid(z)
        log_keep = jnp.where(mask, log_beta - z, 0.0)
        suffix = lax.cumsum(log_keep, axis=3, reverse=True) - log_keep
        w = jnp.where(mask, jnp.exp(log_beta + suffix), 0.0)
        return jnp.einsum('bhqk,bhkd->bhqd', w.astype(vt.dtype), vt)

    o = lax.map(block, (qb, q_pos))
    return o.transpose(1, 0, 3, 2, 4).reshape(b, s, h * dh)


def rg_lru_branch(xb, gate, conv_w, conv_b, wa, ba, wx, bx, lam):
    b, s, d = xb.shape
    xc = lax.conv_general_dilated(
        xb, conv_w[:, None, :], window_strides=(1,), padding=[(CONV_WIDTH - 1, 0)],
        dimension_numbers=('NWC', 'WIO', 'NWC'), feature_group_count=d) + conv_b
    xh = xc.reshape(b, s, LRU_HEADS, d // LRU_HEADS)
    r = jax.nn.sigmoid(jnp.einsum('bshi,hij->bshj', xh, wa).reshape(b, s, d) + ba)
    i = jax.nn.sigmoid(jnp.einsum('bshi,hij->bshj', xh, wx).reshape(b, s, d) + bx)
    log_a = (-LRU_C * r.astype(jnp.float32)) * jax.nn.softplus(-lam.astype(jnp.float32))
    a = jnp.exp(log_a)
    u = jnp.sqrt(-jnp.expm1(2.0 * log_a)) * (i * xc).astype(jnp.float32)

    def combine(left, right):
        a_l, b_l = left
        a_r, b_r = right
        return a_l * a_r, a_r * b_l + b_r

    _, h = lax.associative_scan(combine, (a, u), axis=1)
    return jax.nn.gelu(gate) * h.astype(xb.dtype)


def multiscale_pool(xc, pool_w, pool_scale):
    b, s, d = xc.shape
    xf = xc.astype(jnp.float32)
    count = jnp.arange(1, s + 1, dtype=jnp.float32)[None, :, None]
    outs = []
    for g, w in enumerate(POOL_WINDOWS):
        xg = xf[..., g * POOL_GROUP:(g + 1) * POOL_GROUP]
        cs = jnp.cumsum(xg, axis=1)
        prev = jnp.pad(cs, ((0, 0), (w, 0), (0, 0)))[:, :s]
        mean = (cs - prev) / jnp.minimum(count, float(w))
        outs.append(mean - xg)
    pooled = jnp.stack(outs, axis=2).astype(xc.dtype)
    y = jnp.einsum('bsgi,gij->bsgj', pooled, pool_w).reshape(b, s, d)
    return y * pool_scale


def setup_inputs(seed: int = 0) -> dict:
    key = jax.random.key(seed)
    ks = iter(jax.random.split(key, 40))

    def w(shape, fan_in):
        return jax.random.normal(next(ks), shape, jnp.float32) * (fan_in ** -0.5)

    def gain(shape):
        return 1.0 + 0.02 * jax.random.normal(next(ks), shape, jnp.float32)

    def bias(shape):
        return 0.02 * jax.random.normal(next(ks), shape, jnp.float32)

    x = jax.random.normal(next(ks), (BATCH, SEQ, D_MODEL), jnp.float32)
    p = jax.random.normal(next(ks), (DEPTH, BATCH, SEQ, PLE_DIM), jnp.float32)
    u = jax.random.uniform(next(ks), (DEPTH, D_LRU), jnp.float32, minval=0.9, maxval=0.999)
    a0 = u ** (1.0 / LRU_C)
    lru_lambda = jnp.log(a0) - jnp.log1p(-a0)
    gh = D_LRU // LRU_HEADS
    return {
        'x': x,
        'p': p,
        'ffn1_norm': gain((DEPTH, D_MODEL)),
        'ffn1_w1': w((DEPTH, D_MODEL, D_FF), D_MODEL),
        'ffn1_w3': w((DEPTH, D_MODEL, D_FF), D_MODEL),
        'ffn1_w2': w((DEPTH, D_FF, D_MODEL), D_FF),
        'mix_norm': gain((DEPTH, D_MODEL)),
        'w_in': w((DEPTH, D_MODEL, D_IN), D_MODEL),
        'q_norm': gain((DEPTH, HEAD_DIM)),
        'k_norm': gain((DEPTH, HEAD_DIM)),
        'conv_w': w((DEPTH, CONV_WIDTH, D_LRU), CONV_WIDTH),
        'conv_b': bias((DEPTH, D_LRU)),
        'lru_wa': w((DEPTH, LRU_HEADS, gh, gh), gh),
        'lru_ba': bias((DEPTH, D_LRU)),
        'lru_wx': w((DEPTH, LRU_HEADS, gh, gh), gh),
        'lru_bx': bias((DEPTH, D_LRU)),
        'lru_lambda': lru_lambda,
        'pool_w': w((DEPTH, len(POOL_WINDOWS), POOL_GROUP, POOL_GROUP), POOL_GROUP),
        'pool_scale': 1.0 + 0.1 * jax.random.normal(next(ks), (DEPTH, D_POOL), jnp.float32),
        'w_out': w((DEPTH, D_MIX, D_MODEL), D_MIX),
        'ffn2_norm': gain((DEPTH, D_MODEL)),
        'ffn2_w1': w((DEPTH, D_MODEL, D_FF), D_MODEL),
        'ffn2_w3': w((DEPTH, D_MODEL, D_FF), D_MODEL),
        'ffn2_w2': w((DEPTH, D_FF, D_MODEL), D_FF),
        'ple_norm': gain((DEPTH, D_MODEL)),
        'ple_gate_w': w((DEPTH, D_MODEL, D_MODEL), D_MODEL),
        'ple_proj': w((DEPTH, PLE_DIM, D_MODEL), PLE_DIM),
    }


def reference(x, p, ffn1_norm, ffn1_w1, ffn1_w3, ffn1_w2, mix_norm, w_in, q_norm, k_norm,
              conv_w, conv_b, lru_wa, lru_ba, lru_wx, lru_bx, lru_lambda, pool_w, pool_scale,
              w_out, ffn2_norm, ffn2_w1, ffn2_w3, ffn2_w2, ple_norm, ple_gate_w, ple_proj):
    b, s, _ = x.shape
    splits = [D_SB, 2 * D_SB, 3 * D_SB, 3 * D_SB + D_LRU, 3 * D_SB + 2 * D_LRU]
    for i in range(DEPTH):
        x = x + 0.5 * swiglu_ffn(x, ffn1_norm[i], ffn1_w1[i], ffn1_w3[i], ffn1_w2[i])
        h = rms_norm(x, mix_norm[i])
        proj = h @ w_in[i]
        q, k, v, xb, gb, xc = jnp.split(proj, splits, axis=-1)
        q = rms_norm(q.reshape(b, s, SB_HEADS, HEAD_DIM), q_norm[i])
        k = rms_norm(k.reshape(b, s, SB_HEADS, HEAD_DIM), k_norm[i])
        v = v.reshape(b, s, SB_HEADS, HEAD_DIM)
        o_sb = stick_breaking_attention(q, k, v)
        o_lru = rg_lru_branch(xb, gb, conv_w[i], conv_b[i], lru_wa[i], lru_ba[i],
                              lru_wx[i], lru_bx[i], lru_lambda[i])
        o_pool = multiscale_pool(xc, pool_w[i], pool_scale[i])
        x = x + jnp.concatenate([o_sb, o_lru, o_pool], axis=-1) @ w_out[i]
        x = x + 0.5 * swiglu_ffn(x, ffn2_norm[i], ffn2_w1[i], ffn2_w3[i], ffn2_w2[i])
        gate = jax.nn.sigmoid(rms_norm(x, ple_norm[i]) @ ple_gate_w[i])
        x = x + gate * (p[i] @ ple_proj[i])
    return x
```

```python
import functools

import jax
import jax.numpy as jnp
from jax import lax
from jax.experimental import pallas as pl
from jax.experimental.pallas import tpu as pltpu

EPS = 1e-6
HEAD_DIM = 64
POOL_WINDOWS = (2, 4, 8, 16)
CONV_WIDTH = 4
LRU_C = 8.0

LANES = 128
TOKEN_TILE = 512
ATTN_TILE = 256
FFN_CHUNK = 1408
CONV_TAIL = 8
POOL_TAIL = 16
VMEM_LIMIT = 60 * 1024 * 1024

F32 = jnp.float32
BF16 = jnp.bfloat16


def _dot(a, b):
    return jnp.dot(a, b, preferred_element_type=F32)


def _rms(x, g):
    ms = jnp.mean(x * x, axis=-1, keepdims=True)
    return x * lax.rsqrt(ms + EPS) * g


def _ffn_half_step(x, g_ref, w1_ref, w3_ref, w2_ref):
    h = _rms(x, g_ref[...]).astype(BF16)
    d_ff = w1_ref.shape[1]
    acc = None
    for c0 in range(0, d_ff, FFN_CHUNK):
        c1 = min(c0 + FFN_CHUNK, d_ff)
        a = _dot(h, w1_ref[:, c0:c1])
        b = _dot(h, w3_ref[:, c0:c1])
        m = (a * jax.nn.sigmoid(a) * b).astype(BF16)
        part = _dot(m, w2_ref[c0:c1, :])
        acc = part if acc is None else acc + part
    return x + 0.5 * acc


def _split_dot(x, w):
    hi = x.astype(BF16)
    lo = (x - hi.astype(F32)).astype(BF16)
    return _dot(hi, w) + _dot(lo, w)


def _head_rms(q, head_mean, g):
    ms = _split_dot(q * q, head_mean)
    return q * lax.rsqrt(ms + EPS) * g


def _shift_rows(ext, shift, tail):
    if shift == 0:
        return ext[tail:]
    return pltpu.roll(ext, shift, axis=0)[tail:]


def _neg_expm1(y):
    poly = 1.0 / 3628800.0
    for k in (362880.0, 40320.0, 5040.0, 720.0, 120.0, 24.0, 6.0, 2.0, 1.0):
        poly = poly * y + 1.0 / k
    series = -y * poly
    return jnp.where(y > -0.25, series, 1.0 - jnp.exp(y))


def _linear_scan(a, u, h0):
    rows = a.shape[0]
    row = lax.broadcasted_iota(jnp.int32, a.shape, 0)
    d = 1
    while d < rows:
        keep = row >= d
        a_prev = jnp.where(keep, pltpu.roll(a, d, axis=0), 1.0)
        u_prev = jnp.where(keep, pltpu.roll(u, d, axis=0), 0.0)
        u = a * u_prev + u
        a = a * a_prev
        d *= 2
    return u + a * h0


def _mix_in_kernel(x_ref, g1_ref, w1_ref, w3_ref, w2_ref, gm_ref, win_ref, qg_ref, kg_ref, hmean_ref,
                   convw_ref, convb_ref, wgate_ref, bgate_ref, lam_ref, poolw_ref, pscale_ref, pwin_ref,
                   x1_ref, q_ref, k_ref, v_ref, mix_ref,
                   xb_tail, xc_tail, h_carry, *, d_sb, d_lru, d_pool):
    s = pl.program_id(1)
    tm = x_ref.shape[1]

    @pl.when(s == 0)
    def _():
        xb_tail[...] = jnp.zeros_like(xb_tail)
        xc_tail[...] = jnp.zeros_like(xc_tail)
        h_carry[...] = jnp.zeros_like(h_carry)

    x1 = _ffn_half_step(x_ref[0], g1_ref, w1_ref, w3_ref, w2_ref)
    x1_ref[0] = x1

    h = _rms(x1, gm_ref[...]).astype(BF16)
    o = 0
    q = _dot(h, win_ref[:, o:o + d_sb]); o += d_sb
    k = _dot(h, win_ref[:, o:o + d_sb]); o += d_sb
    v = _dot(h, win_ref[:, o:o + d_sb]); o += d_sb
    xb = _dot(h, win_ref[:, o:o + d_lru]); o += d_lru
    gb = _dot(h, win_ref[:, o:o + d_lru]); o += d_lru
    xc = _dot(h, win_ref[:, o:o + d_pool])

    hmean = hmean_ref[...]
    q_ref[0] = (_head_rms(q, hmean, qg_ref[...]) * (HEAD_DIM ** -0.5)).astype(BF16)
    k_ref[0] = _head_rms(k, hmean, kg_ref[...]).astype(BF16)
    v_ref[0] = v.astype(BF16)

    ext = jnp.concatenate([xb_tail[...], xb], axis=0)
    conv = convb_ref[...] + convw_ref[CONV_WIDTH - 1:CONV_WIDTH, :] * xb
    for j in range(1, CONV_WIDTH):
        conv = conv + convw_ref[CONV_WIDTH - 1 - j:CONV_WIDTH - j, :] * _shift_rows(ext, j, CONV_TAIL)
    xb_tail[...] = xb[tm - CONV_TAIL:, :]
    gates = jax.nn.sigmoid(_dot(conv.astype(BF16), wgate_ref[...]) + bgate_ref[...])
    r_gate = gates[:, :d_lru]
    i_gate = gates[:, d_lru:]
    lam = lam_ref[...]
    softplus_neg_lam = jnp.maximum(-lam, 0.0) + jnp.log1p(jnp.exp(-jnp.abs(lam)))
    log_a = (-LRU_C * r_gate) * softplus_neg_lam
    a = jnp.exp(log_a)
    u = jnp.sqrt(_neg_expm1(2.0 * log_a)) * (i_gate * conv)
    hseq = _linear_scan(a, u, h_carry[...])
    h_carry[...] = hseq[tm - 1:tm, :]
    mix_ref[0, :, 0:d_lru] = (jax.nn.gelu(gb) * hseq).astype(BF16)

    ext = jnp.concatenate([xc_tail[...], xc], axis=0)
    lane = lax.broadcasted_iota(jnp.int32, ext.shape, 1)
    group = d_pool // len(POOL_WINDOWS)
    run = ext
    wsum = None
    for gi, w in enumerate(POOL_WINDOWS):
        run = run + pltpu.roll(run, w // 2, axis=0)
        wsum = run if wsum is None else jnp.where(lane >= gi * group, run, wsum)
    wsum = wsum[POOL_TAIL:]
    xc_tail[...] = xc[tm - POOL_TAIL:, :]
    pos = (s * tm + 1 + lax.broadcasted_iota(jnp.int32, xc.shape, 0)).astype(F32)
    count = jnp.minimum(pos, pwin_ref[...])
    pooled = wsum / count - xc
    mix_ref[0, :, d_lru:d_lru + d_pool] = (_dot(pooled.astype(BF16), poolw_ref[...]) * pscale_ref[...]).astype(BF16)


def _attn_kernel(q_ref, k_ref, v_ref, tri_ref, o_ref):
    i = pl.program_id(2)
    tq = q_ref.shape[1]
    q = q_ref[0]
    tri = tri_ref[...]
    lane = lax.broadcasted_iota(jnp.int32, (tq, LANES), 1)
    row = lax.broadcasted_iota(jnp.int32, (tq, tq), 0)
    col = lax.broadcasted_iota(jnp.int32, (tq, tq), 1)
    strictly_before = col < row

    def block(j, acc, carry, qh, diagonal):
        start = pl.multiple_of(j * tq, tq)
        kj = k_ref[0, pl.ds(start, tq), :]
        vj = v_ref[0, pl.ds(start, tq), :]
        z = lax.dot_general(qh, kj, (((1,), (1,)), ((), ())), preferred_element_type=F32)
        sp = jnp.maximum(z, 0.0) + jnp.log(1.0 + jnp.exp(-jnp.abs(z)))
        if diagonal:
            sp = jnp.where(strictly_before, sp, 0.0)
        c = _split_dot(sp, tri)
        w = jnp.exp(z - c - carry)
        if diagonal:
            w = jnp.where(strictly_before, w, 0.0)
        acc = acc + _dot(w.astype(BF16), vj)
        carry = carry + c[:, 0:1]
        return acc, carry

    outs = []
    for hd in range(LANES // HEAD_DIM):
        in_head = (lane >= hd * HEAD_DIM) & (lane < (hd + 1) * HEAD_DIM)
        qh = jnp.where(in_head, q, jnp.zeros_like(q))
        acc, carry = block(i, jnp.zeros((tq, LANES), F32), jnp.zeros((tq, 1), F32), qh, True)
        acc, carry = lax.fori_loop(
            0, i, lambda t, ac, qh=qh: block(i - 1 - t, ac[0], ac[1], qh, False), (acc, carry))
        outs.append(acc)
    o_ref[0] = jnp.where(lane < HEAD_DIM, outs[0], outs[1]).astype(o_ref.dtype)


def _mix_out_kernel(x1_ref, osb_ref, mix_ref, p_ref, wo_ref, g2_ref, w1_ref, w3_ref, w2_ref,
                    gp_ref, wg_ref, wp_ref, out_ref, *, d_sb):
    x = x1_ref[0] + _dot(osb_ref[0], wo_ref[0:d_sb, :]) + _dot(mix_ref[0], wo_ref[d_sb:, :])
    x = _ffn_half_step(x, g2_ref, w1_ref, w3_ref, w2_ref)
    gate = jax.nn.sigmoid(_dot(_rms(x, gp_ref[...]).astype(BF16), wg_ref[...]))
    emb = _dot(p_ref[0].astype(BF16), wp_ref[...])
    out_ref[0] = x + gate * emb


def _block_diag(w):
    l, h, i, j = w.shape
    eye = jnp.eye(h, dtype=w.dtype)
    return jnp.einsum('lhij,hg->lhigj', w, eye).reshape(l, h * i, h * j)


def _layer_spec(shape, layer):
    nd = len(shape) - 1
    return pl.BlockSpec((None,) + tuple(shape[1:]), lambda b, s: (layer,) + (0,) * nd,
                        pipeline_mode=pl.Buffered(1))


def _const_spec(shape):
    nd = len(shape)
    return pl.BlockSpec(tuple(shape), lambda *_: (0,) * nd, pipeline_mode=pl.Buffered(1))


def kernel(x, p, ffn1_norm, ffn1_w1, ffn1_w3, ffn1_w2, mix_norm, w_in, q_norm, k_norm, conv_w, conv_b,
           lru_wa, lru_ba, lru_wx, lru_bx, lru_lambda, pool_w, pool_scale, w_out, ffn2_norm, ffn2_w1,
           ffn2_w3, ffn2_w2, ple_norm, ple_gate_w, ple_proj):
    bsz, seq, d_model = x.shape
    depth = p.shape[0]
    ple_dim = p.shape[-1]
    d_lru = conv_b.shape[-1]
    d_pool = pool_scale.shape[-1]
    d_sb = (w_in.shape[-1] - 2 * d_lru - d_pool) // 3
    n_heads = d_sb // HEAD_DIM
    d_mix = d_lru + d_pool
    tm = min(TOKEN_TILE, seq)
    tq = min(ATTN_TILE, seq)
    assert seq % tm == 0 and seq % tq == 0 and d_sb % LANES == 0

    row = lambda a: a.reshape(depth, 1, a.shape[-1]).astype(F32)
    bf = lambda a: a.astype(BF16)
    g1, gm, g2, gp = row(ffn1_norm), row(mix_norm), row(ffn2_norm), row(ple_norm)
    qg = row(jnp.tile(q_norm, (1, n_heads)))
    kg = row(jnp.tile(k_norm, (1, n_heads)))
    cb, lam, ps = row(conv_b), row(lru_lambda), row(pool_scale)
    bgate = row(jnp.concatenate([lru_ba, lru_bx], axis=-1))
    wgate = bf(jnp.concatenate([_block_diag(lru_wa), _block_diag(lru_wx)], axis=-1))
    poolw = bf(_block_diag(pool_w))
    f1w1, f1w3, f1w2 = bf(ffn1_w1), bf(ffn1_w3), bf(ffn1_w2)
    f2w1, f2w3, f2w2 = bf(ffn2_w1), bf(ffn2_w3), bf(ffn2_w2)
    win, wo, wg, wp = bf(w_in), bf(w_out), bf(ple_gate_w), bf(ple_proj)
    convw = conv_w.astype(F32)
    head_mean = bf(jnp.kron(jnp.eye(n_heads, dtype=F32), jnp.full((HEAD_DIM, HEAD_DIM), 1.0 / HEAD_DIM, F32)))
    pool_win = jnp.repeat(jnp.asarray(POOL_WINDOWS, F32), d_pool // len(POOL_WINDOWS)).reshape(1, d_pool)
    idx = jnp.arange(tq)
    tri = (idx[:, None] >= idx[None, :]).astype(BF16)

    tok_grid = (bsz, seq // tm)
    tok = lambda width: pl.BlockSpec((1, tm, width), lambda b, s: (b, s, 0))
    seq_params = pltpu.CompilerParams(dimension_semantics=("arbitrary", "arbitrary"),
                                      vmem_limit_bytes=VMEM_LIMIT)

    for layer in range(depth):
        lspec = functools.partial(_layer_spec, layer=layer)
        x1, q, k, v, mix = pl.pallas_call(
            functools.partial(_mix_in_kernel, d_sb=d_sb, d_lru=d_lru, d_pool=d_pool),
            grid=tok_grid,
            in_specs=[tok(d_model), lspec(g1.shape), lspec(f1w1.shape), lspec(f1w3.shape), lspec(f1w2.shape),
                      lspec(gm.shape), lspec(win.shape), lspec(qg.shape), lspec(kg.shape),
                      _const_spec(head_mean.shape), lspec(convw.shape), lspec(cb.shape), lspec(wgate.shape),
                      lspec(bgate.shape), lspec(lam.shape), lspec(poolw.shape), lspec(ps.shape),
                      _const_spec(pool_win.shape)],
            out_specs=[tok(d_model), tok(d_sb), tok(d_sb), tok(d_sb), tok(d_mix)],
            out_shape=[jax.ShapeDtypeStruct((bsz, seq, d_model), F32),
                       jax.ShapeDtypeStruct((bsz, seq, d_sb), BF16),
                       jax.ShapeDtypeStruct((bsz, seq, d_sb), BF16),
                       jax.ShapeDtypeStruct((bsz, seq, d_sb), BF16),
                       jax.ShapeDtypeStruct((bsz, seq, d_mix), BF16)],
            scratch_shapes=[pltpu.VMEM((CONV_TAIL, d_lru), F32), pltpu.VMEM((POOL_TAIL, d_pool), F32),
                            pltpu.VMEM((1, d_lru), F32)],
            compiler_params=seq_params,
            name=f"mix_in_{layer}",
        )(x, g1, f1w1, f1w3, f1w2, gm, win, qg, kg, head_mean, convw, cb, wgate, bgate, lam, poolw, ps,
          pool_win)

        osb = pl.pallas_call(
            _attn_kernel,
            grid=(bsz, d_sb // LANES, seq // tq),
            in_specs=[pl.BlockSpec((1, tq, LANES), lambda b, hp, i: (b, i, hp)),
                      pl.BlockSpec((1, seq, LANES), lambda b, hp, i: (b, 0, hp)),
                      pl.BlockSpec((1, seq, LANES), lambda b, hp, i: (b, 0, hp)),
                      pl.BlockSpec((tq, tq), lambda b, hp, i: (0, 0))],
            out_specs=pl.BlockSpec((1, tq, LANES), lambda b, hp, i: (b, i, hp)),
            out_shape=jax.ShapeDtypeStruct((bsz, seq, d_sb), BF16),
            compiler_params=pltpu.CompilerParams(
                dimension_semantics=("arbitrary", "arbitrary", "arbitrary"), vmem_limit_bytes=VMEM_LIMIT),
            name=f"attn_{layer}",
        )(q, k, v, tri)

        x = pl.pallas_call(
            functools.partial(_mix_out_kernel, d_sb=d_sb),
            grid=tok_grid,
            in_specs=[tok(d_model), tok(d_sb), tok(d_mix),
                      pl.BlockSpec((None, 1, tm, ple_dim), lambda b, s, layer=layer: (layer, b, s, 0)),
                      lspec(wo.shape), lspec(g2.shape), lspec(f2w1.shape), lspec(f2w3.shape),
                      lspec(f2w2.shape), lspec(gp.shape), lspec(wg.shape), lspec(wp.shape)],
            out_specs=tok(d_model),
            out_shape=jax.ShapeDtypeStruct((bsz, seq, d_model), F32),
            compiler_params=seq_params,
            name=f"mix_out_{layer}",
        )(x1, osb, mix, p, wo, g2, f2w1, f2w3, f2w2, gp, wg, wp)
    return x
```

```python
import functools

import jax
import jax.numpy as jnp
from jax import lax
from jax.experimental import pallas as pl
from jax.experimental.pallas import tpu as pltpu

EPS = 1e-6
HEAD_DIM = 64
POOL_WINDOWS = (2, 4, 8, 16)
CONV_WIDTH = 4
LRU_C = 8.0

LANES = 128
TOKEN_TILE = 512
ATTN_Q_TILE = 512
ATTN_K_TILE = 256
SOFTPLUS_CLAMP = 80.0
FFN_CHUNK = 1408
CONV_TAIL = 8
POOL_TAIL = 16
VMEM_LIMIT = 60 * 1024 * 1024

F32 = jnp.float32
BF16 = jnp.bfloat16


def _dot(a, b):
    return jnp.dot(a, b, preferred_element_type=F32)


def _rms(x, g):
    ms = jnp.mean(x * x, axis=-1, keepdims=True)
    return x * lax.rsqrt(ms + EPS) * g


def _ffn_half_step(x, g_ref, w1_ref, w3_ref, w2_ref):
    h = _rms(x, g_ref[...]).astype(BF16)
    d_ff = w1_ref.shape[1]
    acc = None
    for c0 in range(0, d_ff, FFN_CHUNK):
        c1 = min(c0 + FFN_CHUNK, d_ff)
        a = _dot(h, w1_ref[:, c0:c1])
        b = _dot(h, w3_ref[:, c0:c1])
        m = (a * jax.nn.sigmoid(a) * b).astype(BF16)
        part = _dot(m, w2_ref[c0:c1, :])
        acc = part if acc is None else acc + part
    return x + 0.5 * acc


def _split_dot(x, w):
    hi = x.astype(BF16)
    lo = (x - hi.astype(F32)).astype(BF16)
    return _dot(hi, w) + _dot(lo, w)


def _head_rms(q, head_mean, g):
    ms = _split_dot(q * q, head_mean)
    return q * lax.rsqrt(ms + EPS) * g


def _shift_rows(ext, shift, tail):
    if shift == 0:
        return ext[tail:]
    return pltpu.roll(ext, shift, axis=0)[tail:]


def _neg_expm1(y):
    poly = 1.0 / 3628800.0
    for k in (362880.0, 40320.0, 5040.0, 720.0, 120.0, 24.0, 6.0, 2.0, 1.0):
        poly = poly * y + 1.0 / k
    series = -y * poly
    return jnp.where(y > -0.25, series, 1.0 - jnp.exp(y))


def _linear_scan(a, u, h0):
    rows = a.shape[0]
    row = lax.broadcasted_iota(jnp.int32, a.shape, 0)
    d = 1
    while d < rows:
        keep = row >= d
        a_prev = jnp.where(keep, pltpu.roll(a, d, axis=0), 1.0)
        u_prev = jnp.where(keep, pltpu.roll(u, d, axis=0), 0.0)
        u = a * u_prev + u
        a = a * a_prev
        d *= 2
    return u + a * h0


def _mix_in_kernel(x_ref, g1_ref, w1_ref, w3_ref, w2_ref, gm_ref, win_ref, qg_ref, kg_ref, hmean_ref,
                   convw_ref, convb_ref, wgate_ref, bgate_ref, lam_ref, poolw_ref, pscale_ref, pwin_ref,
                   x1_ref, q_ref, k_ref, v_ref, mix_ref,
                   xb_tail, xc_tail, h_carry, *, d_sb, d_lru, d_pool):
    s = pl.program_id(1)
    tm = x_ref.shape[1]

    @pl.when(s == 0)
    def _():
        xb_tail[...] = jnp.zeros_like(xb_tail)
        xc_tail[...] = jnp.zeros_like(xc_tail)
        h_carry[...] = jnp.zeros_like(h_carry)

    x1 = _ffn_half_step(x_ref[0], g1_ref, w1_ref, w3_ref, w2_ref)
    x1_ref[0] = x1

    h = _rms(x1, gm_ref[...]).astype(BF16)
    o = 0
    q = _dot(h, win_ref[:, o:o + d_sb]); o += d_sb
    k = _dot(h, win_ref[:, o:o + d_sb]); o += d_sb
    v = _dot(h, win_ref[:, o:o + d_sb]); o += d_sb
    xb = _dot(h, win_ref[:, o:o + d_lru]); o += d_lru
    gb = _dot(h, win_ref[:, o:o + d_lru]); o += d_lru
    xc = _dot(h, win_ref[:, o:o + d_pool])

    hmean = hmean_ref[...]
    q_ref[0] = (_head_rms(q, hmean, qg_ref[...]) * (HEAD_DIM ** -0.5)).astype(BF16)
    k_ref[0] = _head_rms(k, hmean, kg_ref[...]).astype(BF16)
    v_ref[0] = v.astype(BF16)

    ext = jnp.concatenate([xb_tail[...], xb], axis=0)
    conv = convb_ref[...] + convw_ref[CONV_WIDTH - 1:CONV_WIDTH, :] * xb
    for j in range(1, CONV_WIDTH):
        conv = conv + convw_ref[CONV_WIDTH - 1 - j:CONV_WIDTH - j, :] * _shift_rows(ext, j, CONV_TAIL)
    xb_tail[...] = xb[tm - CONV_TAIL:, :]
    gates = jax.nn.sigmoid(_dot(conv.astype(BF16), wgate_ref[...]) + bgate_ref[...])
    r_gate = gates[:, :d_lru]
    i_gate = gates[:, d_lru:]
    lam = lam_ref[...]
    softplus_neg_lam = jnp.maximum(-lam, 0.0) + jnp.log1p(jnp.exp(-jnp.abs(lam)))
    log_a = (-LRU_C * r_gate) * softplus_neg_lam
    a = jnp.exp(log_a)
    u = jnp.sqrt(_neg_expm1(2.0 * log_a)) * (i_gate * conv)
    hseq = _linear_scan(a, u, h_carry[...])
    h_carry[...] = hseq[tm - 1:tm, :]
    mix_ref[0, :, 0:d_lru] = (jax.nn.gelu(gb) * hseq).astype(BF16)

    ext = jnp.concatenate([xc_tail[...], xc], axis=0)
    lane = lax.broadcasted_iota(jnp.int32, ext.shape, 1)
    group = d_pool // len(POOL_WINDOWS)
    run = ext
    wsum = None
    for gi, w in enumerate(POOL_WINDOWS):
        run = run + pltpu.roll(run, w // 2, axis=0)
        wsum = run if wsum is None else jnp.where(lane >= gi * group, run, wsum)
    wsum = wsum[POOL_TAIL:]
    xc_tail[...] = xc[tm - POOL_TAIL:, :]
    pos = (s * tm + 1 + lax.broadcasted_iota(jnp.int32, xc.shape, 0)).astype(F32)
    count = jnp.minimum(pos, pwin_ref[...])
    pooled = wsum / count - xc
    mix_ref[0, :, d_lru:d_lru + d_pool] = (_dot(pooled.astype(BF16), poolw_ref[...]) * pscale_ref[...]).astype(BF16)


def _attn_kernel(q_ref, k_ref, v_ref, tri_ref, o_ref, acc_ref, carry_ref):
    i = pl.program_id(2)
    tq = q_ref.shape[1]
    tk = tri_ref.shape[1]
    heads = LANES // HEAD_DIM
    rows = heads * tq
    q = q_ref[0]
    lane = lax.broadcasted_iota(jnp.int32, (tq, LANES), 1)
    qs = jnp.concatenate(
        [jnp.where((lane >= hd * HEAD_DIM) & (lane < (hd + 1) * HEAD_DIM), q, jnp.zeros_like(q))
         for hd in range(heads)], axis=0)
    tri = tri_ref[...]
    acc_ref[...] = jnp.zeros_like(acc_ref)
    carry_ref[...] = jnp.zeros_like(carry_ref)

    def key_block(jb, diagonal):
        start = pl.multiple_of(jb * tq, tq)
        kj = k_ref[0, pl.ds(start, tq), :]
        vj = v_ref[0, pl.ds(start, tq), :]
        z = lax.dot_general(qs, kj, (((1,), (1,)), ((), ())), preferred_element_type=F32)
        sp = jnp.maximum(z, jnp.log(1.0 + jnp.exp(jnp.minimum(z, SOFTPLUS_CLAMP))))
        if diagonal:
            qpos = lax.rem(lax.broadcasted_iota(jnp.int32, (rows, tq), 0), tq)
            kpos = lax.broadcasted_iota(jnp.int32, (rows, tq), 1)
            strictly_before = kpos < qpos
            sp = jnp.where(strictly_before, sp, 0.0)
        sp = sp.astype(BF16)
        carry = carry_ref[...]
        ws = []
        for d in reversed(range(tq // tk)):
            cols = slice(d * tk, (d + 1) * tk)
            c = _dot(sp[:, cols], tri)
            ws.insert(0, jnp.exp(z[:, cols] - c - carry))
            carry = carry + c[:, 0:1]
        w = jnp.concatenate(ws, axis=1)
        if diagonal:
            w = jnp.where(strictly_before, w, 0.0)
        acc_ref[...] += _dot(w.astype(BF16), vj)
        carry_ref[...] = carry

    key_block(i, True)

    def body(t, _):
        key_block(i - 1 - t, False)
        return 0

    lax.fori_loop(0, i, body, 0)
    acc = acc_ref[...]
    out = acc[0:tq]
    for hd in range(1, heads):
        out = jnp.where(lane >= hd * HEAD_DIM, acc[hd * tq:(hd + 1) * tq], out)
    o_ref[0] = out.astype(o_ref.dtype)


def _mix_out_kernel(x1_ref, osb_ref, mix_ref, p_ref, wo_ref, g2_ref, w1_ref, w3_ref, w2_ref,
                    gp_ref, wg_ref, wp_ref, out_ref, *, d_sb):
    x = x1_ref[0] + _dot(osb_ref[0], wo_ref[0:d_sb, :]) + _dot(mix_ref[0], wo_ref[d_sb:, :])
    x = _ffn_half_step(x, g2_ref, w1_ref, w3_ref, w2_ref)
    gate = jax.nn.sigmoid(_dot(_rms(x, gp_ref[...]).astype(BF16), wg_ref[...]))
    emb = _dot(p_ref[0].astype(BF16), wp_ref[...])
    out_ref[0] = x + gate * emb


def _block_diag(w):
    l, h, i, j = w.shape
    eye = jnp.eye(h, dtype=w.dtype)
    return jnp.einsum('lhij,hg->lhigj', w, eye).reshape(l, h * i, h * j)


def _layer_spec(shape, layer):
    nd = len(shape) - 1
    return pl.BlockSpec((None,) + tuple(shape[1:]), lambda b, s: (layer,) + (0,) * nd,
                        pipeline_mode=pl.Buffered(1))


def _const_spec(shape):
    nd = len(shape)
    return pl.BlockSpec(tuple(shape), lambda *_: (0,) * nd, pipeline_mode=pl.Buffered(1))


def kernel(x, p, ffn1_norm, ffn1_w1, ffn1_w3, ffn1_w2, mix_norm, w_in, q_norm, k_norm, conv_w, conv_b,
           lru_wa, lru_ba, lru_wx, lru_bx, lru_lambda, pool_w, pool_scale, w_out, ffn2_norm, ffn2_w1,
           ffn2_w3, ffn2_w2, ple_norm, ple_gate_w, ple_proj):
    bsz, seq, d_model = x.shape
    depth = p.shape[0]
    ple_dim = p.shape[-1]
    d_lru = conv_b.shape[-1]
    d_pool = pool_scale.shape[-1]
    d_sb = (w_in.shape[-1] - 2 * d_lru - d_pool) // 3
    n_heads = d_sb // HEAD_DIM
    d_mix = d_lru + d_pool
    tm = min(TOKEN_TILE, seq)
    tq = min(ATTN_Q_TILE, seq)
    tk = min(ATTN_K_TILE, tq)
    assert seq % tm == 0 and seq % tq == 0 and tq % tk == 0 and d_sb % LANES == 0

    row = lambda a: a.reshape(depth, 1, a.shape[-1]).astype(F32)
    bf = lambda a: a.astype(BF16)
    g1, gm, g2, gp = row(ffn1_norm), row(mix_norm), row(ffn2_norm), row(ple_norm)
    qg = row(jnp.tile(q_norm, (1, n_heads)))
    kg = row(jnp.tile(k_norm, (1, n_heads)))
    cb, lam, ps = row(conv_b), row(lru_lambda), row(pool_scale)
    bgate = row(jnp.concatenate([lru_ba, lru_bx], axis=-1))
    wgate = bf(jnp.concatenate([_block_diag(lru_wa), _block_diag(lru_wx)], axis=-1))
    poolw = bf(_block_diag(pool_w))
    f1w1, f1w3, f1w2 = bf(ffn1_w1), bf(ffn1_w3), bf(ffn1_w2)
    f2w1, f2w3, f2w2 = bf(ffn2_w1), bf(ffn2_w3), bf(ffn2_w2)
    win, wo, wg, wp = bf(w_in), bf(w_out), bf(ple_gate_w), bf(ple_proj)
    convw = conv_w.astype(F32)
    head_mean = bf(jnp.kron(jnp.eye(n_heads, dtype=F32), jnp.full((HEAD_DIM, HEAD_DIM), 1.0 / HEAD_DIM, F32)))
    pool_win = jnp.repeat(jnp.asarray(POOL_WINDOWS, F32), d_pool // len(POOL_WINDOWS)).reshape(1, d_pool)
    idx = jnp.arange(tk)
    tri = (idx[:, None] >= idx[None, :]).astype(BF16)
    attn_rows = (LANES // HEAD_DIM) * tq

    tok_grid = (bsz, seq // tm)
    tok = lambda width: pl.BlockSpec((1, tm, width), lambda b, s: (b, s, 0))
    seq_params = pltpu.CompilerParams(dimension_semantics=("arbitrary", "arbitrary"),
                                      vmem_limit_bytes=VMEM_LIMIT)

    for layer in range(depth):
        lspec = functools.partial(_layer_spec, layer=layer)
        x1, q, k, v, mix = pl.pallas_call(
            functools.partial(_mix_in_kernel, d_sb=d_sb, d_lru=d_lru, d_pool=d_pool),
            grid=tok_grid,
            in_specs=[tok(d_model), lspec(g1.shape), lspec(f1w1.shape), lspec(f1w3.shape), lspec(f1w2.shape),
                      lspec(gm.shape), lspec(win.shape), lspec(qg.shape), lspec(kg.shape),
                      _const_spec(head_mean.shape), lspec(convw.shape), lspec(cb.shape), lspec(wgate.shape),
                      lspec(bgate.shape), lspec(lam.shape), lspec(poolw.shape), lspec(ps.shape),
                      _const_spec(pool_win.shape)],
            out_specs=[tok(d_model), tok(d_sb), tok(d_sb), tok(d_sb), tok(d_mix)],
            out_shape=[jax.ShapeDtypeStruct((bsz, seq, d_model), F32),
                       jax.ShapeDtypeStruct((bsz, seq, d_sb), BF16),
                       jax.ShapeDtypeStruct((bsz, seq, d_sb), BF16),
                       jax.ShapeDtypeStruct((bsz, seq, d_sb), BF16),
                       jax.ShapeDtypeStruct((bsz, seq, d_mix), BF16)],
            scratch_shapes=[pltpu.VMEM((CONV_TAIL, d_lru), F32), pltpu.VMEM((POOL_TAIL, d_pool), F32),
                            pltpu.VMEM((1, d_lru), F32)],
            compiler_params=seq_params,
            name=f"mix_in_{layer}",
        )(x, g1, f1w1, f1w3, f1w2, gm, win, qg, kg, head_mean, convw, cb, wgate, bgate, lam, poolw, ps,
          pool_win)

        osb = pl.pallas_call(
            _attn_kernel,
            grid=(bsz, d_sb // LANES, seq // tq),
            in_specs=[pl.BlockSpec((1, tq, LANES), lambda b, hp, i: (b, i, hp)),
                      pl.BlockSpec((1, seq, LANES), lambda b, hp, i: (b, 0, hp)),
                      pl.BlockSpec((1, seq, LANES), lambda b, hp, i: (b, 0, hp)),
                      pl.BlockSpec((tk, tk), lambda b, hp, i: (0, 0))],
            out_specs=pl.BlockSpec((1, tq, LANES), lambda b, hp, i: (b, i, hp)),
            out_shape=jax.ShapeDtypeStruct((bsz, seq, d_sb), BF16),
            scratch_shapes=[pltpu.VMEM((attn_rows, LANES), F32), pltpu.VMEM((attn_rows, 1), F32)],
            compiler_params=pltpu.CompilerParams(
                dimension_semantics=("arbitrary", "arbitrary", "arbitrary"), vmem_limit_bytes=VMEM_LIMIT),
            name=f"attn_{layer}",
        )(q, k, v, tri)

        x = pl.pallas_call(
            functools.partial(_mix_out_kernel, d_sb=d_sb),
            grid=tok_grid,
            in_specs=[tok(d_model), tok(d_sb), tok(d_mix),
                      pl.BlockSpec((None, 1, tm, ple_dim), lambda b, s, layer=layer: (layer, b, s, 0)),
                      lspec(wo.shape), lspec(g2.shape), lspec(f2w1.shape), lspec(f2w3.shape),
                      lspec(f2w2.shape), lspec(gp.shape), lspec(wg.shape), lspec(wp.shape)],
            out_specs=tok(d_model),
            out_shape=jax.ShapeDtypeStruct((bsz, seq, d_model), F32),
            compiler_params=seq_params,
            name=f"mix_out_{layer}",
        )(x1, osb, mix, p, wo, g2, f2w1, f2w3, f2w2, gp, wg, wp)
    return x
```

```python
import functools

import jax
import jax.numpy as jnp
from jax import lax
from jax.experimental import pallas as pl
from jax.experimental.pallas import tpu as pltpu

EPS = 1e-6
HEAD_DIM = 64
POOL_WINDOWS = (2, 4, 8, 16)
CONV_WIDTH = 4
LRU_C = 8.0

LANES = 128
MXU_TILE = 256
TOKEN_TILE = 512
ATTN_Q_TILE = 512
SOFTPLUS_CLAMP = 80.0
FFN_CHUNK = 1408
CONV_TAIL = 8
POOL_TAIL = 16
VMEM_LIMIT = 60 * 1024 * 1024

F32 = jnp.float32
BF16 = jnp.bfloat16


def _dot(a, b):
    return jnp.dot(a, b, preferred_element_type=F32)


def _rms(x, g):
    ms = jnp.mean(x * x, axis=-1, keepdims=True)
    return x * lax.rsqrt(ms + EPS) * g


def _ffn_half_step(x, g_ref, w1_ref, w3_ref, w2_ref):
    h = _rms(x, g_ref[...]).astype(BF16)
    d_ff = w1_ref.shape[1]
    acc = None
    for c0 in range(0, d_ff, FFN_CHUNK):
        c1 = min(c0 + FFN_CHUNK, d_ff)
        a = _dot(h, w1_ref[:, c0:c1])
        b = _dot(h, w3_ref[:, c0:c1])
        m = (a * jax.nn.sigmoid(a) * b).astype(BF16)
        part = _dot(m, w2_ref[c0:c1, :])
        acc = part if acc is None else acc + part
    return x + 0.5 * acc


def _head_rms(q, head_mean_ref, g):
    sq = (q * q).astype(BF16)
    width = q.shape[1]
    parts = []
    for c0 in range(0, width, MXU_TILE):
        c1 = min(c0 + MXU_TILE, width)
        parts.append(_dot(sq[:, c0:c1], head_mean_ref[c0:c1, c0:c1]))
    ms = jnp.concatenate(parts, axis=1)
    return q * lax.rsqrt(ms + EPS) * g


def _shift_rows(ext, shift, tail):
    if shift == 0:
        return ext[tail:]
    return pltpu.roll(ext, shift, axis=0)[tail:]


def _neg_expm1(y):
    poly = 1.0 / 3628800.0
    for k in (362880.0, 40320.0, 5040.0, 720.0, 120.0, 24.0, 6.0, 2.0, 1.0):
        poly = poly * y + 1.0 / k
    series = -y * poly
    return jnp.where(y > -0.25, series, 1.0 - jnp.exp(y))


def _linear_scan(a, u, h0):
    rows = a.shape[0]
    row = lax.broadcasted_iota(jnp.int32, a.shape, 0)
    d = 1
    while d < rows:
        keep = row >= d
        a_prev = jnp.where(keep, pltpu.roll(a, d, axis=0), 1.0)
        u_prev = jnp.where(keep, pltpu.roll(u, d, axis=0), 0.0)
        u = a * u_prev + u
        a = a * a_prev
        d *= 2
    return u + a * h0


def _mix_in_kernel(x_ref, g1_ref, w1_ref, w3_ref, w2_ref, gm_ref, win_ref, qg_ref, kg_ref, hmean_ref,
                   convw_ref, convb_ref, wgate_ref, bgate_ref, lam_ref, poolw_ref, pscale_ref, pwin_ref,
                   x1_ref, q_ref, k_ref, v_ref, mix_ref,
                   xb_tail, xc_tail, h_carry, *, d_sb, d_lru, d_pool):
    s = pl.program_id(1)
    tm = x_ref.shape[1]

    @pl.when(s == 0)
    def _():
        xb_tail[...] = jnp.zeros_like(xb_tail)
        xc_tail[...] = jnp.zeros_like(xc_tail)
        h_carry[...] = jnp.zeros_like(h_carry)

    x1 = _ffn_half_step(x_ref[0], g1_ref, w1_ref, w3_ref, w2_ref)
    x1_ref[0] = x1

    h = _rms(x1, gm_ref[...]).astype(BF16)
    proj = _dot(h, win_ref[...])
    o = 0
    q = proj[:, o:o + d_sb]; o += d_sb
    k = proj[:, o:o + d_sb]; o += d_sb
    v = proj[:, o:o + d_sb]; o += d_sb
    xb = proj[:, o:o + d_lru]; o += d_lru
    gb = proj[:, o:o + d_lru]; o += d_lru
    xc = proj[:, o:o + d_pool]

    q_ref[0] = (_head_rms(q, hmean_ref, qg_ref[...]) * (HEAD_DIM ** -0.5)).astype(BF16)
    k_ref[0] = _head_rms(k, hmean_ref, kg_ref[...]).astype(BF16)
    v_ref[0] = v.astype(BF16)

    ext = jnp.concatenate([xb_tail[...], xb], axis=0)
    conv = convb_ref[...] + convw_ref[CONV_WIDTH - 1:CONV_WIDTH, :] * xb
    for j in range(1, CONV_WIDTH):
        conv = conv + convw_ref[CONV_WIDTH - 1 - j:CONV_WIDTH - j, :] * _shift_rows(ext, j, CONV_TAIL)
    xb_tail[...] = xb[tm - CONV_TAIL:, :]
    gates = jax.nn.sigmoid(_dot(conv.astype(BF16), wgate_ref[...]) + bgate_ref[...])
    r_gate = gates[:, :d_lru]
    i_gate = gates[:, d_lru:]
    lam = lam_ref[...]
    softplus_neg_lam = jnp.maximum(-lam, 0.0) + jnp.log1p(jnp.exp(-jnp.abs(lam)))
    log_a = (-LRU_C * r_gate) * softplus_neg_lam
    a = jnp.exp(log_a)
    u = jnp.sqrt(_neg_expm1(2.0 * log_a)) * (i_gate * conv)
    hseq = _linear_scan(a, u, h_carry[...])
    h_carry[...] = hseq[tm - 1:tm, :]
    mix_ref[0, :, 0:d_lru] = (jax.nn.gelu(gb) * hseq).astype(BF16)

    ext = jnp.concatenate([xc_tail[...], xc], axis=0)
    lane = lax.broadcasted_iota(jnp.int32, ext.shape, 1)
    group = d_pool // len(POOL_WINDOWS)
    run = ext
    wsum = None
    for gi, w in enumerate(POOL_WINDOWS):
        run = run + pltpu.roll(run, w // 2, axis=0)
        wsum = run if wsum is None else jnp.where(lane >= gi * group, run, wsum)
    wsum = wsum[POOL_TAIL:]
    xc_tail[...] = xc[tm - POOL_TAIL:, :]
    pos = (s * tm + 1 + lax.broadcasted_iota(jnp.int32, xc.shape, 0)).astype(F32)
    count = jnp.minimum(pos, pwin_ref[...])
    pooled = wsum / count - xc
    mix_ref[0, :, d_lru:d_lru + d_pool] = (_dot(pooled.astype(BF16), poolw_ref[...]) * pscale_ref[...]).astype(BF16)


def _attn_kernel(q_ref, k_ref, v_ref, tri_ref, o_ref, acc_ref, carry_ref, wa_ref, wb_ref, *z_refs):
    i = pl.program_id(2)
    tq = q_ref.shape[1]
    tk = tri_ref.shape[1]
    heads = LANES // HEAD_DIM
    rows = heads * tq
    q = q_ref[0]
    lane = lax.broadcasted_iota(jnp.int32, (tq, LANES), 1)
    qs = jnp.concatenate(
        [jnp.where((lane >= hd * HEAD_DIM) & (lane < (hd + 1) * HEAD_DIM), q, jnp.zeros_like(q))
         for hd in range(heads)], axis=0)
    tri = tri_ref[...]
    acc_ref[...] = jnp.zeros_like(acc_ref)
    carry_ref[...] = jnp.zeros_like(carry_ref)

    def keys(ref, j):
        return ref[0, pl.ds(pl.multiple_of(j * tk, tk), tk), :]

    def scores(j, z_ref):
        z_ref[...] = lax.dot_general(qs, keys(k_ref, j), (((1,), (1,)), ((), ())),
                                     preferred_element_type=F32)

    def weights(z_ref, w_ref, diagonal_offset=None):
        z = z_ref[...]
        sp = jnp.maximum(z, jnp.log(1.0 + jnp.exp(jnp.minimum(z, SOFTPLUS_CLAMP))))
        if diagonal_offset is not None:
            qpos = lax.rem(lax.broadcasted_iota(jnp.int32, (rows, tk), 0), tq)
            kpos = diagonal_offset + lax.broadcasted_iota(jnp.int32, (rows, tk), 1)
            strictly_before = kpos < qpos
            sp = jnp.where(strictly_before, sp, 0.0)
        c = _dot(sp.astype(BF16), tri)
        carry = carry_ref[...]
        w = jnp.exp(z_ref[...] - c - carry)
        if diagonal_offset is not None:
            w = jnp.where(strictly_before, w, 0.0)
        w_ref[...] = w.astype(BF16)
        carry_ref[...] = carry + c[:, 0:1]

    def accumulate(j, w_ref):
        acc_ref[...] += _dot(w_ref[...], keys(v_ref, j))

    top = 2 * i + 1
    w_refs = (wa_ref, wb_ref)
    n_z = len(z_refs)
    scores(top, z_refs[2])
    scores(top - 1, z_refs[3])
    scores(jnp.maximum(top - 2, 0), z_refs[0])
    scores(jnp.maximum(top - 3, 0), z_refs[1])
    weights(z_refs[2], wa_ref, diagonal_offset=tk)
    accumulate(top, wa_ref)
    weights(z_refs[3], wb_ref, diagonal_offset=0)

    def run(ja, count, prefetch):
        for s in range(count):
            accumulate(ja - s + 1, w_refs[(s + 1) % 2])
            if prefetch:
                scores(jnp.maximum(ja - s - 2, 0), z_refs[(s + 2) % n_z])
            weights(z_refs[s % n_z], w_refs[s % 2])

    def body(u, _):
        run(top - 2 - n_z * u, n_z, True)
        return 0

    rounds = (2 * i) // n_z
    lax.fori_loop(0, rounds, body, 0)

    @pl.when(2 * i - n_z * rounds > 0)
    def _():
        run(top - 2 - n_z * rounds, 2, False)

    accumulate(0, wb_ref)
    acc = acc_ref[...]
    out = acc[0:tq]
    for hd in range(1, heads):
        out = jnp.where(lane >= hd * HEAD_DIM, acc[hd * tq:(hd + 1) * tq], out)
    o_ref[0] = out.astype(o_ref.dtype)


def _mix_out_kernel(x1_ref, osb_ref, mix_ref, p_ref, wo_ref, g2_ref, w1_ref, w3_ref, w2_ref,
                    gp_ref, wg_ref, wp_ref, out_ref, *, d_sb):
    x = x1_ref[0] + _dot(osb_ref[0], wo_ref[0:d_sb, :]) + _dot(mix_ref[0], wo_ref[d_sb:, :])
    x = _ffn_half_step(x, g2_ref, w1_ref, w3_ref, w2_ref)
    gate = jax.nn.sigmoid(_dot(_rms(x, gp_ref[...]).astype(BF16), wg_ref[...]))
    emb = _dot(p_ref[0].astype(BF16), wp_ref[...])
    out_ref[0] = x + gate * emb


def _block_diag(w):
    l, h, i, j = w.shape
    eye = jnp.eye(h, dtype=w.dtype)
    return jnp.einsum('lhij,hg->lhigj', w, eye).reshape(l, h * i, h * j)


def _layer_spec(shape, layer):
    nd = len(shape) - 1
    return pl.BlockSpec((None,) + tuple(shape[1:]), lambda b, s: (layer,) + (0,) * nd,
                        pipeline_mode=pl.Buffered(1))


def _const_spec(shape):
    nd = len(shape)
    return pl.BlockSpec(tuple(shape), lambda *_: (0,) * nd, pipeline_mode=pl.Buffered(1))


def kernel(x, p, ffn1_norm, ffn1_w1, ffn1_w3, ffn1_w2, mix_norm, w_in, q_norm, k_norm, conv_w, conv_b,
           lru_wa, lru_ba, lru_wx, lru_bx, lru_lambda, pool_w, pool_scale, w_out, ffn2_norm, ffn2_w1,
           ffn2_w3, ffn2_w2, ple_norm, ple_gate_w, ple_proj):
    bsz, seq, d_model = x.shape
    depth = p.shape[0]
    ple_dim = p.shape[-1]
    d_lru = conv_b.shape[-1]
    d_pool = pool_scale.shape[-1]
    d_sb = (w_in.shape[-1] - 2 * d_lru - d_pool) // 3
    n_heads = d_sb // HEAD_DIM
    d_mix = d_lru + d_pool
    tm = min(TOKEN_TILE, seq)
    tq = min(ATTN_Q_TILE, seq)
    tk = tq // 2
    assert seq % tm == 0 and seq % tq == 0 and tk % 8 == 0 and d_sb % LANES == 0

    row = lambda a: a.reshape(depth, 1, a.shape[-1]).astype(F32)
    bf = lambda a: a.astype(BF16)
    g1, gm, g2, gp = row(ffn1_norm), row(mix_norm), row(ffn2_norm), row(ple_norm)
    qg = row(jnp.tile(q_norm, (1, n_heads)))
    kg = row(jnp.tile(k_norm, (1, n_heads)))
    cb, lam, ps = row(conv_b), row(lru_lambda), row(pool_scale)
    bgate = row(jnp.concatenate([lru_ba, lru_bx], axis=-1))
    wgate = bf(jnp.concatenate([_block_diag(lru_wa), _block_diag(lru_wx)], axis=-1))
    poolw = bf(_block_diag(pool_w))
    f1w1, f1w3, f1w2 = bf(ffn1_w1), bf(ffn1_w3), bf(ffn1_w2)
    f2w1, f2w3, f2w2 = bf(ffn2_w1), bf(ffn2_w3), bf(ffn2_w2)
    win, wo, wg, wp = bf(w_in), bf(w_out), bf(ple_gate_w), bf(ple_proj)
    convw = conv_w.astype(F32)
    head_mean = bf(jnp.kron(jnp.eye(n_heads, dtype=F32), jnp.full((HEAD_DIM, HEAD_DIM), 1.0 / HEAD_DIM, F32)))
    pool_win = jnp.repeat(jnp.asarray(POOL_WINDOWS, F32), d_pool // len(POOL_WINDOWS)).reshape(1, d_pool)
    idx = jnp.arange(tk)
    tri = (idx[:, None] >= idx[None, :]).astype(BF16)
    attn_rows = (LANES // HEAD_DIM) * tq

    tok_grid = (bsz, seq // tm)
    tok = lambda width: pl.BlockSpec((1, tm, width), lambda b, s: (b, s, 0))
    seq_params = pltpu.CompilerParams(dimension_semantics=("arbitrary", "arbitrary"),
                                      vmem_limit_bytes=VMEM_LIMIT)

    for layer in range(depth):
        lspec = functools.partial(_layer_spec, layer=layer)
        x1, q, k, v, mix = pl.pallas_call(
            functools.partial(_mix_in_kernel, d_sb=d_sb, d_lru=d_lru, d_pool=d_pool),
            grid=tok_grid,
            in_specs=[tok(d_model), lspec(g1.shape), lspec(f1w1.shape), lspec(f1w3.shape), lspec(f1w2.shape),
                      lspec(gm.shape), lspec(win.shape), lspec(qg.shape), lspec(kg.shape),
                      _const_spec(head_mean.shape), lspec(convw.shape), lspec(cb.shape), lspec(wgate.shape),
                      lspec(bgate.shape), lspec(lam.shape), lspec(poolw.shape), lspec(ps.shape),
                      _const_spec(pool_win.shape)],
            out_specs=[tok(d_model), tok(d_sb), tok(d_sb), tok(d_sb), tok(d_mix)],
            out_shape=[jax.ShapeDtypeStruct((bsz, seq, d_model), F32),
                       jax.ShapeDtypeStruct((bsz, seq, d_sb), BF16),
                       jax.ShapeDtypeStruct((bsz, seq, d_sb), BF16),
                       jax.ShapeDtypeStruct((bsz, seq, d_sb), BF16),
                       jax.ShapeDtypeStruct((bsz, seq, d_mix), BF16)],
            scratch_shapes=[pltpu.VMEM((CONV_TAIL, d_lru), F32), pltpu.VMEM((POOL_TAIL, d_pool), F32),
                            pltpu.VMEM((1, d_lru), F32)],
            compiler_params=seq_params,
            name=f"mix_in_{layer}",
        )(x, g1, f1w1, f1w3, f1w2, gm, win, qg, kg, head_mean, convw, cb, wgate, bgate, lam, poolw, ps,
          pool_win)

        osb = pl.pallas_call(
            _attn_kernel,
            grid=(bsz, d_sb // LANES, seq // tq),
            in_specs=[pl.BlockSpec((1, tq, LANES), lambda b, hp, i: (b, i, hp)),
                      pl.BlockSpec((1, seq, LANES), lambda b, hp, i: (b, 0, hp)),
                      pl.BlockSpec((1, seq, LANES), lambda b, hp, i: (b, 0, hp)),
                      pl.BlockSpec((tk, tk), lambda b, hp, i: (0, 0))],
            out_specs=pl.BlockSpec((1, tq, LANES), lambda b, hp, i: (b, i, hp)),
            out_shape=jax.ShapeDtypeStruct((bsz, seq, d_sb), BF16),
            scratch_shapes=[pltpu.VMEM((attn_rows, LANES), F32), pltpu.VMEM((attn_rows, 1), F32),
                            pltpu.VMEM((attn_rows, tk), BF16), pltpu.VMEM((attn_rows, tk), BF16)]
                           + [pltpu.VMEM((attn_rows, tk), F32)] * 4,
            compiler_params=pltpu.CompilerParams(
                dimension_semantics=("arbitrary", "arbitrary", "arbitrary"), vmem_limit_bytes=VMEM_LIMIT),
            name=f"attn_{layer}",
        )(q, k, v, tri)

        x = pl.pallas_call(
            functools.partial(_mix_out_kernel, d_sb=d_sb),
            grid=tok_grid,
            in_specs=[tok(d_model), tok(d_sb), tok(d_mix),
                      pl.BlockSpec((None, 1, tm, ple_dim), lambda b, s, layer=layer: (layer, b, s, 0)),
                      lspec(wo.shape), lspec(g2.shape), lspec(f2w1.shape), lspec(f2w3.shape),
                      lspec(f2w2.shape), lspec(gp.shape), lspec(wg.shape), lspec(wp.shape)],
            out_specs=tok(d_model),
            out_shape=jax.ShapeDtypeStruct((bsz, seq, d_model), F32),
            compiler_params=seq_params,
            name=f"mix_out_{layer}",
        )(x1, osb, mix, p, wo, g2, f2w1, f2w3, f2w2, gp, wg, wp)
    return x
```

```python
import functools

import jax
import jax.numpy as jnp
from jax import lax
from jax.experimental import pallas as pl
from jax.experimental.pallas import tpu as pltpu

EPS = 1e-6
HEAD_DIM = 64
POOL_WINDOWS = (2, 4, 8, 16)
CONV_WIDTH = 4
LRU_C = 8.0

LANES = 128
MXU_TILE = 256
TOKEN_TILE = 512
ATTN_Q_TILE = 512
LOG2E = 1.4426950408889634
SOFTPLUS_CLAMP = 115.0
FFN_CHUNK = 6 * MXU_TILE
CONV_TAIL = 8
POOL_TAIL = 16
VMEM_LIMIT = 60 * 1024 * 1024

F32 = jnp.float32
BF16 = jnp.bfloat16


def _dot(a, b):
    return jnp.dot(a, b, preferred_element_type=F32)


def _rms(x, g):
    ms = jnp.mean(x * x, axis=-1, keepdims=True)
    return x * lax.rsqrt(ms + EPS) * g


def _ffn_half_step(x, g_ref, w1_ref, w3_ref, w2_ref):
    h = _rms(x, g_ref[...]).astype(BF16)
    d_ff = w1_ref.shape[1]
    acc = None
    for c0 in range(0, d_ff, FFN_CHUNK):
        c1 = min(c0 + FFN_CHUNK, d_ff)
        a = _dot(h, w1_ref[:, c0:c1])
        b = _dot(h, w3_ref[:, c0:c1])
        m = (a * jax.nn.sigmoid(a) * b).astype(BF16)
        part = _dot(m, w2_ref[c0:c1, :])
        acc = part if acc is None else acc + part
    return x + 0.5 * acc


def _head_rms(q, head_mean_ref, g):
    sq = (q * q).astype(BF16)
    width = q.shape[1]
    parts = []
    for c0 in range(0, width, MXU_TILE):
        c1 = min(c0 + MXU_TILE, width)
        parts.append(_dot(sq[:, c0:c1], head_mean_ref[c0:c1, c0:c1]))
    ms = jnp.concatenate(parts, axis=1)
    return q * lax.rsqrt(ms + EPS) * g


def _shift_rows(ext, shift, tail):
    if shift == 0:
        return ext[tail:]
    return pltpu.roll(ext, shift, axis=0)[tail:]


def _neg_expm1(y):
    poly = 1.0 / 3628800.0
    for k in (362880.0, 40320.0, 5040.0, 720.0, 120.0, 24.0, 6.0, 2.0, 1.0):
        poly = poly * y + 1.0 / k
    series = -y * poly
    return jnp.where(y > -0.25, series, 1.0 - jnp.exp(y))


def _linear_scan(a, u, h0):
    rows = a.shape[0]
    row = lax.broadcasted_iota(jnp.int32, a.shape, 0)
    d = 1
    while d < rows:
        keep = row >= d
        a_prev = jnp.where(keep, pltpu.roll(a, d, axis=0), 1.0)
        u_prev = jnp.where(keep, pltpu.roll(u, d, axis=0), 0.0)
        u = a * u_prev + u
        a = a * a_prev
        d *= 2
    return u + a * h0


def _mix_in_kernel(x_ref, g1_ref, w1_ref, w3_ref, w2_ref, gm_ref, win_ref, qg_ref, kg_ref, hmean_ref,
                   convw_ref, convb_ref, wgate_ref, bgate_ref, lam_ref, poolw_ref, pscale_ref, pwin_ref,
                   x1_ref, q_ref, k_ref, v_ref, mix_ref,
                   xb_tail, xc_tail, h_carry, *, d_sb, d_lru, d_pool):
    s = pl.program_id(1)
    tm = x_ref.shape[1]

    @pl.when(s == 0)
    def _():
        xb_tail[...] = jnp.zeros_like(xb_tail)
        xc_tail[...] = jnp.zeros_like(xc_tail)
        h_carry[...] = jnp.zeros_like(h_carry)

    x1 = _ffn_half_step(x_ref[0], g1_ref, w1_ref, w3_ref, w2_ref)
    x1_ref[0] = x1

    h = _rms(x1, gm_ref[...]).astype(BF16)
    proj = _dot(h, win_ref[...])
    o = 0
    q = proj[:, o:o + d_sb]; o += d_sb
    k = proj[:, o:o + d_sb]; o += d_sb
    v = proj[:, o:o + d_sb]; o += d_sb
    xb = proj[:, o:o + d_lru]; o += d_lru
    gb = proj[:, o:o + d_lru]; o += d_lru
    xc = proj[:, o:o + d_pool]

    q_ref[0] = (_head_rms(q, hmean_ref, qg_ref[...]) * (HEAD_DIM ** -0.5 * LOG2E)).astype(BF16)
    k_ref[0] = _head_rms(k, hmean_ref, kg_ref[...]).astype(BF16)
    v_ref[0] = v.astype(BF16)

    ext = jnp.concatenate([xb_tail[...], xb], axis=0)
    conv = convb_ref[...] + convw_ref[CONV_WIDTH - 1:CONV_WIDTH, :] * xb
    for j in range(1, CONV_WIDTH):
        conv = conv + convw_ref[CONV_WIDTH - 1 - j:CONV_WIDTH - j, :] * _shift_rows(ext, j, CONV_TAIL)
    xb_tail[...] = xb[tm - CONV_TAIL:, :]
    gates = jax.nn.sigmoid(_dot(conv.astype(BF16), wgate_ref[...]) + bgate_ref[...])
    r_gate = gates[:, :d_lru]
    i_gate = gates[:, d_lru:]
    lam = lam_ref[...]
    softplus_neg_lam = jnp.maximum(-lam, 0.0) + jnp.log1p(jnp.exp(-jnp.abs(lam)))
    log_a = (-LRU_C * r_gate) * softplus_neg_lam
    a = jnp.exp(log_a)
    u = jnp.sqrt(_neg_expm1(2.0 * log_a)) * (i_gate * conv)
    hseq = _linear_scan(a, u, h_carry[...])
    h_carry[...] = hseq[tm - 1:tm, :]
    mix_ref[0, :, 0:d_lru] = (jax.nn.gelu(gb) * hseq).astype(BF16)

    ext = jnp.concatenate([xc_tail[...], xc], axis=0)
    lane = lax.broadcasted_iota(jnp.int32, ext.shape, 1)
    group = d_pool // len(POOL_WINDOWS)
    run = ext
    wsum = None
    for gi, w in enumerate(POOL_WINDOWS):
        run = run + pltpu.roll(run, w // 2, axis=0)
        wsum = run if wsum is None else jnp.where(lane >= gi * group, run, wsum)
    wsum = wsum[POOL_TAIL:]
    xc_tail[...] = xc[tm - POOL_TAIL:, :]
    pos = (s * tm + 1 + lax.broadcasted_iota(jnp.int32, xc.shape, 0)).astype(F32)
    count = jnp.minimum(pos, pwin_ref[...])
    pooled = wsum / count - xc
    mix_ref[0, :, d_lru:d_lru + d_pool] = (_dot(pooled.astype(BF16), poolw_ref[...]) * pscale_ref[...]).astype(BF16)


def _attn_kernel(q_ref, k_ref, v_ref, tri_ref, o_ref, acc_ref, carry_ref, wa_ref, wb_ref, *z_refs):
    i = pl.program_id(2)
    tq = q_ref.shape[1]
    tk = tri_ref.shape[1]
    heads = LANES // HEAD_DIM
    rows = heads * tq
    q = q_ref[0]
    lane = lax.broadcasted_iota(jnp.int32, (tq, LANES), 1)
    qs = jnp.concatenate(
        [jnp.where((lane >= hd * HEAD_DIM) & (lane < (hd + 1) * HEAD_DIM), q, jnp.zeros_like(q))
         for hd in range(heads)], axis=0)
    tri = tri_ref[...]
    acc_ref[...] = jnp.zeros_like(acc_ref)
    carry_ref[...] = jnp.zeros_like(carry_ref)

    def keys(ref, j):
        return ref[0, pl.ds(pl.multiple_of(j * tk, tk), tk), :]

    def scores(j, z_ref):
        z_ref[...] = lax.dot_general(qs, keys(k_ref, j), (((1,), (1,)), ((), ())),
                                     preferred_element_type=F32)

    def weights(z_ref, w_ref, diagonal_offset=None):
        z = z_ref[...]
        sp = jnp.maximum(z, jnp.log(1.0 + jnp.exp2(jnp.minimum(z, SOFTPLUS_CLAMP))) * LOG2E)
        if diagonal_offset is not None:
            qpos = lax.rem(lax.broadcasted_iota(jnp.int32, (rows, tk), 0), tq)
            kpos = diagonal_offset + lax.broadcasted_iota(jnp.int32, (rows, tk), 1)
            strictly_before = kpos < qpos
            sp = jnp.where(strictly_before, sp, 0.0)
        c = _dot(sp.astype(BF16), tri)
        carry = carry_ref[...]
        w = jnp.exp2(z_ref[...] - c - carry)
        if diagonal_offset is not None:
            w = jnp.where(strictly_before, w, 0.0)
        w_ref[...] = w.astype(BF16)
        carry_ref[...] = carry + c[:, 0:1]

    def accumulate(j, w_ref):
        acc_ref[...] += _dot(w_ref[...], keys(v_ref, j))

    top = 2 * i + 1
    w_refs = (wa_ref, wb_ref)
    n_z = len(z_refs)
    scores(top, z_refs[2])
    scores(top - 1, z_refs[3])
    scores(jnp.maximum(top - 2, 0), z_refs[0])
    scores(jnp.maximum(top - 3, 0), z_refs[1])
    weights(z_refs[2], wa_ref, diagonal_offset=tk)
    accumulate(top, wa_ref)
    weights(z_refs[3], wb_ref, diagonal_offset=0)

    def run(ja, count, prefetch):
        for s in range(count):
            accumulate(ja - s + 1, w_refs[(s + 1) % 2])
            if prefetch:
                scores(jnp.maximum(ja - s - 2, 0), z_refs[(s + 2) % n_z])
            weights(z_refs[s % n_z], w_refs[s % 2])

    def body(u, _):
        run(top - 2 - n_z * u, n_z, True)
        return 0

    rounds = (2 * i) // n_z
    lax.fori_loop(0, rounds, body, 0)

    @pl.when(2 * i - n_z * rounds > 0)
    def _():
        run(top - 2 - n_z * rounds, 2, False)

    accumulate(0, wb_ref)
    acc = acc_ref[...]
    out = acc[0:tq]
    for hd in range(1, heads):
        out = jnp.where(lane >= hd * HEAD_DIM, acc[hd * tq:(hd + 1) * tq], out)
    o_ref[0] = out.astype(o_ref.dtype)


def _mix_out_kernel(x1_ref, osb_ref, mix_ref, p_ref, wo_ref, g2_ref, w1_ref, w3_ref, w2_ref,
                    gp_ref, wg_ref, wp_ref, out_ref):
    mixed = jnp.concatenate([osb_ref[0], mix_ref[0]], axis=1)
    x = x1_ref[0] + _dot(mixed, wo_ref[...])
    x = _ffn_half_step(x, g2_ref, w1_ref, w3_ref, w2_ref)
    gate = jax.nn.sigmoid(_dot(_rms(x, gp_ref[...]).astype(BF16), wg_ref[...]))
    emb = _dot(p_ref[0].astype(BF16), wp_ref[...])
    out_ref[0] = x + gate * emb


def _block_diag(w):
    l, h, i, j = w.shape
    eye = jnp.eye(h, dtype=w.dtype)
    return jnp.einsum('lhij,hg->lhigj', w, eye).reshape(l, h * i, h * j)


def _layer_spec(shape, layer):
    nd = len(shape) - 1
    return pl.BlockSpec((None,) + tuple(shape[1:]), lambda b, s: (layer,) + (0,) * nd,
                        pipeline_mode=pl.Buffered(1))


def _const_spec(shape):
    nd = len(shape)
    return pl.BlockSpec(tuple(shape), lambda *_: (0,) * nd, pipeline_mode=pl.Buffered(1))


def kernel(x, p, ffn1_norm, ffn1_w1, ffn1_w3, ffn1_w2, mix_norm, w_in, q_norm, k_norm, conv_w, conv_b,
           lru_wa, lru_ba, lru_wx, lru_bx, lru_lambda, pool_w, pool_scale, w_out, ffn2_norm, ffn2_w1,
           ffn2_w3, ffn2_w2, ple_norm, ple_gate_w, ple_proj):
    bsz, seq, d_model = x.shape
    depth = p.shape[0]
    ple_dim = p.shape[-1]
    d_lru = conv_b.shape[-1]
    d_pool = pool_scale.shape[-1]
    d_sb = (w_in.shape[-1] - 2 * d_lru - d_pool) // 3
    n_heads = d_sb // HEAD_DIM
    d_mix = d_lru + d_pool
    tm = min(TOKEN_TILE, seq)
    tq = min(ATTN_Q_TILE, seq)
    tk = tq // 2
    assert seq % tm == 0 and seq % tq == 0 and tk % 8 == 0 and d_sb % LANES == 0

    row = lambda a: a.reshape(depth, 1, a.shape[-1]).astype(F32)
    bf = lambda a: a.astype(BF16)
    g1, gm, g2, gp = row(ffn1_norm), row(mix_norm), row(ffn2_norm), row(ple_norm)
    qg = row(jnp.tile(q_norm, (1, n_heads)))
    kg = row(jnp.tile(k_norm, (1, n_heads)))
    cb, lam, ps = row(conv_b), row(lru_lambda), row(pool_scale)
    bgate = row(jnp.concatenate([lru_ba, lru_bx], axis=-1))
    wgate = bf(jnp.concatenate([_block_diag(lru_wa), _block_diag(lru_wx)], axis=-1))
    poolw = bf(_block_diag(pool_w))
    f1w1, f1w3, f1w2 = bf(ffn1_w1), bf(ffn1_w3), bf(ffn1_w2)
    f2w1, f2w3, f2w2 = bf(ffn2_w1), bf(ffn2_w3), bf(ffn2_w2)
    win, wo, wg, wp = bf(w_in), bf(w_out), bf(ple_gate_w), bf(ple_proj)
    convw = conv_w.astype(F32)
    head_mean = bf(jnp.kron(jnp.eye(n_heads, dtype=F32), jnp.full((HEAD_DIM, HEAD_DIM), 1.0 / HEAD_DIM, F32)))
    pool_win = jnp.repeat(jnp.asarray(POOL_WINDOWS, F32), d_pool // len(POOL_WINDOWS)).reshape(1, d_pool)
    idx = jnp.arange(tk)
    tri = (idx[:, None] >= idx[None, :]).astype(BF16)
    attn_rows = (LANES // HEAD_DIM) * tq

    tok_grid = (bsz, seq // tm)
    tok = lambda width: pl.BlockSpec((1, tm, width), lambda b, s: (b, s, 0))
    seq_params = pltpu.CompilerParams(dimension_semantics=("arbitrary", "arbitrary"),
                                      vmem_limit_bytes=VMEM_LIMIT)

    for layer in range(depth):
        lspec = functools.partial(_layer_spec, layer=layer)
        x1, q, k, v, mix = pl.pallas_call(
            functools.partial(_mix_in_kernel, d_sb=d_sb, d_lru=d_lru, d_pool=d_pool),
            grid=tok_grid,
            in_specs=[tok(d_model), lspec(g1.shape), lspec(f1w1.shape), lspec(f1w3.shape), lspec(f1w2.shape),
                      lspec(gm.shape), lspec(win.shape), lspec(qg.shape), lspec(kg.shape),
                      _const_spec(head_mean.shape), lspec(convw.shape), lspec(cb.shape), lspec(wgate.shape),
                      lspec(bgate.shape), lspec(lam.shape), lspec(poolw.shape), lspec(ps.shape),
                      _const_spec(pool_win.shape)],
            out_specs=[tok(d_model), tok(d_sb), tok(d_sb), tok(d_sb), tok(d_mix)],
            out_shape=[jax.ShapeDtypeStruct((bsz, seq, d_model), F32),
                       jax.ShapeDtypeStruct((bsz, seq, d_sb), BF16),
                       jax.ShapeDtypeStruct((bsz, seq, d_sb), BF16),
                       jax.ShapeDtypeStruct((bsz, seq, d_sb), BF16),
                       jax.ShapeDtypeStruct((bsz, seq, d_mix), BF16)],
            scratch_shapes=[pltpu.VMEM((CONV_TAIL, d_lru), F32), pltpu.VMEM((POOL_TAIL, d_pool), F32),
                            pltpu.VMEM((1, d_lru), F32)],
            compiler_params=seq_params,
            name=f"mix_in_{layer}",
        )(x, g1, f1w1, f1w3, f1w2, gm, win, qg, kg, head_mean, convw, cb, wgate, bgate, lam, poolw, ps,
          pool_win)

        osb = pl.pallas_call(
            _attn_kernel,
            grid=(bsz, d_sb // LANES, seq // tq),
            in_specs=[pl.BlockSpec((1, tq, LANES), lambda b, hp, i: (b, i, hp)),
                      pl.BlockSpec((1, seq, LANES), lambda b, hp, i: (b, 0, hp)),
                      pl.BlockSpec((1, seq, LANES), lambda b, hp, i: (b, 0, hp)),
                      pl.BlockSpec((tk, tk), lambda b, hp, i: (0, 0))],
            out_specs=pl.BlockSpec((1, tq, LANES), lambda b, hp, i: (b, i, hp)),
            out_shape=jax.ShapeDtypeStruct((bsz, seq, d_sb), BF16),
            scratch_shapes=[pltpu.VMEM((attn_rows, LANES), F32), pltpu.VMEM((attn_rows, 1), F32),
                            pltpu.VMEM((attn_rows, tk), BF16), pltpu.VMEM((attn_rows, tk), BF16)]
                           + [pltpu.VMEM((attn_rows, tk), F32)] * 4,
            compiler_params=pltpu.CompilerParams(
                dimension_semantics=("arbitrary", "arbitrary", "arbitrary"), vmem_limit_bytes=VMEM_LIMIT),
            name=f"attn_{layer}",
        )(q, k, v, tri)

        x = pl.pallas_call(
            _mix_out_kernel,
            grid=tok_grid,
            in_specs=[tok(d_model), tok(d_sb), tok(d_mix),
                      pl.BlockSpec((None, 1, tm, ple_dim), lambda b, s, layer=layer: (layer, b, s, 0)),
                      lspec(wo.shape), lspec(g2.shape), lspec(f2w1.shape), lspec(f2w3.shape),
                      lspec(f2w2.shape), lspec(gp.shape), lspec(wg.shape), lspec(wp.shape)],
            out_specs=tok(d_model),
            out_shape=jax.ShapeDtypeStruct((bsz, seq, d_model), F32),
            compiler_params=seq_params,
            name=f"mix_out_{layer}",
        )(x1, osb, mix, p, wo, g2, f2w1, f2w3, f2w2, gp, wg, wp)
    return x
```

```python
import functools

import jax
import jax.numpy as jnp
from jax import lax
from jax.experimental import pallas as pl
from jax.experimental.pallas import tpu as pltpu

EPS = 1e-6
HEAD_DIM = 64
POOL_WINDOWS = (2, 4, 8, 16)
CONV_WIDTH = 4
LRU_C = 8.0

LANES = 128
MXU_TILE = 256
TOKEN_TILE = 512
ATTN_Q_TILE = 512
LOG2E = 1.4426950408889634
UNDERFLOW_BITS = 1100.0
SOFTPLUS_CLAMP = 115.0
FFN_CHUNK = 6 * MXU_TILE
CONV_TAIL = 8
POOL_TAIL = 16
VMEM_LIMIT = 60 * 1024 * 1024

F32 = jnp.float32
BF16 = jnp.bfloat16


def _dot(a, b):
    return jnp.dot(a, b, preferred_element_type=F32)


def _rms(x, g):
    ms = jnp.mean(x * x, axis=-1, keepdims=True)
    return x * lax.rsqrt(ms + EPS) * g


def _ffn_half_step(x, g_ref, w1_ref, w3_ref, w2_ref):
    h = _rms(x, g_ref[...]).astype(BF16)
    d_ff = w1_ref.shape[1]
    acc = None
    for c0 in range(0, d_ff, FFN_CHUNK):
        c1 = min(c0 + FFN_CHUNK, d_ff)
        a = _dot(h, w1_ref[:, c0:c1])
        b = _dot(h, w3_ref[:, c0:c1])
        m = (a * jax.nn.sigmoid(a) * b).astype(BF16)
        part = _dot(m, w2_ref[c0:c1, :])
        acc = part if acc is None else acc + part
    return x + 0.5 * acc


def _head_rms(q, head_mean_ref, g):
    sq = (q * q).astype(BF16)
    width = q.shape[1]
    parts = []
    for c0 in range(0, width, MXU_TILE):
        c1 = min(c0 + MXU_TILE, width)
        parts.append(_dot(sq[:, c0:c1], head_mean_ref[c0:c1, c0:c1]))
    ms = jnp.concatenate(parts, axis=1)
    return q * lax.rsqrt(ms + EPS) * g


def _shift_rows(ext, shift, tail):
    if shift == 0:
        return ext[tail:]
    return pltpu.roll(ext, shift, axis=0)[tail:]


def _neg_expm1(y):
    poly = 1.0 / 3628800.0
    for k in (362880.0, 40320.0, 5040.0, 720.0, 120.0, 24.0, 6.0, 2.0, 1.0):
        poly = poly * y + 1.0 / k
    series = -y * poly
    return jnp.where(y > -0.25, series, 1.0 - jnp.exp(y))


def _linear_scan(a, u, h0):
    rows = a.shape[0]
    row = lax.broadcasted_iota(jnp.int32, a.shape, 0)
    d = 1
    while d < rows:
        keep = row >= d
        a_prev = jnp.where(keep, pltpu.roll(a, d, axis=0), 1.0)
        u_prev = jnp.where(keep, pltpu.roll(u, d, axis=0), 0.0)
        u = a * u_prev + u
        a = a * a_prev
        d *= 2
    return u + a * h0


def _mix_in_kernel(x_ref, g1_ref, w1_ref, w3_ref, w2_ref, gm_ref, win_ref, qg_ref, kg_ref, hmean_ref,
                   convw_ref, convb_ref, wgate_ref, bgate_ref, lam_ref, poolw_ref, pscale_ref, pwin_ref,
                   x1_ref, q_ref, k_ref, v_ref, mix_ref,
                   xb_tail, xc_tail, h_carry, *, d_sb, d_lru, d_pool):
    s = pl.program_id(1)
    tm = x_ref.shape[1]

    @pl.when(s == 0)
    def _():
        xb_tail[...] = jnp.zeros_like(xb_tail)
        xc_tail[...] = jnp.zeros_like(xc_tail)
        h_carry[...] = jnp.zeros_like(h_carry)

    x1 = _ffn_half_step(x_ref[0], g1_ref, w1_ref, w3_ref, w2_ref)
    x1_ref[0] = x1

    h = _rms(x1, gm_ref[...]).astype(BF16)
    proj = _dot(h, win_ref[...])
    o = 0
    q = proj[:, o:o + d_sb]; o += d_sb
    k = proj[:, o:o + d_sb]; o += d_sb
    v = proj[:, o:o + d_sb]; o += d_sb
    xb = proj[:, o:o + d_lru]; o += d_lru
    gb = proj[:, o:o + d_lru]; o += d_lru
    xc = proj[:, o:o + d_pool]

    q_ref[0] = (_head_rms(q, hmean_ref, qg_ref[...]) * (HEAD_DIM ** -0.5 * LOG2E)).astype(BF16)
    k_ref[0] = _head_rms(k, hmean_ref, kg_ref[...]).astype(BF16)
    v_ref[0] = v.astype(BF16)

    ext = jnp.concatenate([xb_tail[...], xb], axis=0)
    conv = convb_ref[...] + convw_ref[CONV_WIDTH - 1:CONV_WIDTH, :] * xb
    for j in range(1, CONV_WIDTH):
        conv = conv + convw_ref[CONV_WIDTH - 1 - j:CONV_WIDTH - j, :] * _shift_rows(ext, j, CONV_TAIL)
    xb_tail[...] = xb[tm - CONV_TAIL:, :]
    gates = jax.nn.sigmoid(_dot(conv.astype(BF16), wgate_ref[...]) + bgate_ref[...])
    r_gate = gates[:, :d_lru]
    i_gate = gates[:, d_lru:]
    lam = lam_ref[...]
    softplus_neg_lam = jnp.maximum(-lam, 0.0) + jnp.log1p(jnp.exp(-jnp.abs(lam)))
    log_a = (-LRU_C * r_gate) * softplus_neg_lam
    a = jnp.exp(log_a)
    u = jnp.sqrt(_neg_expm1(2.0 * log_a)) * (i_gate * conv)
    hseq = _linear_scan(a, u, h_carry[...])
    h_carry[...] = hseq[tm - 1:tm, :]
    mix_ref[0, :, 0:d_lru] = (jax.nn.gelu(gb) * hseq).astype(BF16)

    ext = jnp.concatenate([xc_tail[...], xc], axis=0)
    lane = lax.broadcasted_iota(jnp.int32, ext.shape, 1)
    group = d_pool // len(POOL_WINDOWS)
    run = ext
    wsum = None
    for gi, w in enumerate(POOL_WINDOWS):
        run = run + pltpu.roll(run, w // 2, axis=0)
        wsum = run if wsum is None else jnp.where(lane >= gi * group, run, wsum)
    wsum = wsum[POOL_TAIL:]
    xc_tail[...] = xc[tm - POOL_TAIL:, :]
    pos = (s * tm + 1 + lax.broadcasted_iota(jnp.int32, xc.shape, 0)).astype(F32)
    count = jnp.minimum(pos, pwin_ref[...])
    pooled = wsum / count - xc
    mix_ref[0, :, d_lru:d_lru + d_pool] = (_dot(pooled.astype(BF16), poolw_ref[...]) * pscale_ref[...]).astype(BF16)


def _attn_kernel(q_ref, k_ref, v_ref, tri_ref, limit_ref, o_ref, acc_ref, carry_ref, wa_ref, wb_ref, *z_refs,
                 layer):
    i = pl.program_id(2)
    tq = q_ref.shape[1]
    tk = tri_ref.shape[1]
    heads = LANES // HEAD_DIM
    rows = heads * tq
    q = q_ref[0]
    lane = lax.broadcasted_iota(jnp.int32, (tq, LANES), 1)
    qs = jnp.concatenate(
        [jnp.where((lane >= hd * HEAD_DIM) & (lane < (hd + 1) * HEAD_DIM), q, jnp.zeros_like(q))
         for hd in range(heads)], axis=0)
    tri = tri_ref[...]
    acc_ref[...] = jnp.zeros_like(acc_ref)
    carry_ref[...] = jnp.zeros_like(carry_ref)

    def keys(ref, j):
        return ref[0, pl.ds(pl.multiple_of(j * tk, tk), tk), :]

    def scores(j, z_ref):
        z_ref[...] = lax.dot_general(qs, keys(k_ref, j), (((1,), (1,)), ((), ())),
                                     preferred_element_type=F32)

    def weights(z_ref, w_ref, diagonal_offset=None):
        z = z_ref[...]
        sp = jnp.maximum(z, jnp.log(1.0 + jnp.exp2(jnp.minimum(z, SOFTPLUS_CLAMP))) * LOG2E)
        if diagonal_offset is not None:
            qpos = lax.rem(lax.broadcasted_iota(jnp.int32, (rows, tk), 0), tq)
            kpos = diagonal_offset + lax.broadcasted_iota(jnp.int32, (rows, tk), 1)
            strictly_before = kpos < qpos
            sp = jnp.where(strictly_before, sp, 0.0)
        c = _dot(sp.astype(BF16), tri)
        carry = carry_ref[...]
        w = jnp.exp2(z_ref[...] - c - carry)
        if diagonal_offset is not None:
            w = jnp.where(strictly_before, w, 0.0)
        w_ref[...] = w.astype(BF16)
        carry_ref[...] = carry + c[:, 0:1]

    def accumulate(j, w_ref):
        acc_ref[...] += _dot(w_ref[...], keys(v_ref, j))

    top = 2 * i + 1
    w_refs = (wa_ref, wb_ref)
    n_z = len(z_refs)
    scores(top, z_refs[2])
    scores(top - 1, z_refs[3])
    scores(jnp.maximum(top - 2, 0), z_refs[0])
    scores(jnp.maximum(top - 3, 0), z_refs[1])
    weights(z_refs[2], wa_ref, diagonal_offset=tk)
    accumulate(top, wa_ref)
    weights(z_refs[3], wb_ref, diagonal_offset=0)

    def run(ja, count, prefetch):
        for s in range(count):
            accumulate(ja - s + 1, w_refs[(s + 1) % 2])
            if prefetch:
                scores(jnp.maximum(ja - s - 2, 0), z_refs[(s + 2) % n_z])
            weights(z_refs[s % n_z], w_refs[s % 2])

    rounds = (2 * i) // n_z
    limit = limit_ref[layer]

    def more(state):
        u, live = state
        return jnp.logical_and(u < rounds, live > 0)

    def body(state):
        u, _ = state
        run(top - 2 - n_z * u, n_z, True)
        return u + 1, (jnp.min(carry_ref[...]) < limit).astype(jnp.int32)

    done_rounds, live = lax.while_loop(more, body, (jnp.int32(0), jnp.int32(1)))
    tail = jnp.logical_and(live > 0, 2 * i - n_z * rounds > 0)

    @pl.when(tail)
    def _():
        run(top - 2 - n_z * rounds, 2, False)

    accumulate(top - 1 - n_z * done_rounds - 2 * tail.astype(jnp.int32), wb_ref)
    acc = acc_ref[...]
    out = acc[0:tq]
    for hd in range(1, heads):
        out = jnp.where(lane >= hd * HEAD_DIM, acc[hd * tq:(hd + 1) * tq], out)
    o_ref[0] = out.astype(o_ref.dtype)


def _mix_out_kernel(x1_ref, osb_ref, mix_ref, p_ref, wo_ref, g2_ref, w1_ref, w3_ref, w2_ref,
                    gp_ref, wg_ref, wp_ref, out_ref):
    mixed = jnp.concatenate([osb_ref[0], mix_ref[0]], axis=1)
    x = x1_ref[0] + _dot(mixed, wo_ref[...])
    x = _ffn_half_step(x, g2_ref, w1_ref, w3_ref, w2_ref)
    gate = jax.nn.sigmoid(_dot(_rms(x, gp_ref[...]).astype(BF16), wg_ref[...]))
    emb = _dot(p_ref[0].astype(BF16), wp_ref[...])
    out_ref[0] = x + gate * emb


def _block_diag(w):
    l, h, i, j = w.shape
    eye = jnp.eye(h, dtype=w.dtype)
    return jnp.einsum('lhij,hg->lhigj', w, eye).reshape(l, h * i, h * j)


def _layer_spec(shape, layer):
    nd = len(shape) - 1
    return pl.BlockSpec((None,) + tuple(shape[1:]), lambda b, s: (layer,) + (0,) * nd,
                        pipeline_mode=pl.Buffered(1))


def _const_spec(shape):
    nd = len(shape)
    return pl.BlockSpec(tuple(shape), lambda *_: (0,) * nd, pipeline_mode=pl.Buffered(1))


def kernel(x, p, ffn1_norm, ffn1_w1, ffn1_w3, ffn1_w2, mix_norm, w_in, q_norm, k_norm, conv_w, conv_b,
           lru_wa, lru_ba, lru_wx, lru_bx, lru_lambda, pool_w, pool_scale, w_out, ffn2_norm, ffn2_w1,
           ffn2_w3, ffn2_w2, ple_norm, ple_gate_w, ple_proj):
    bsz, seq, d_model = x.shape
    depth = p.shape[0]
    ple_dim = p.shape[-1]
    d_lru = conv_b.shape[-1]
    d_pool = pool_scale.shape[-1]
    d_sb = (w_in.shape[-1] - 2 * d_lru - d_pool) // 3
    n_heads = d_sb // HEAD_DIM
    d_mix = d_lru + d_pool
    tm = min(TOKEN_TILE, seq)
    tq = min(ATTN_Q_TILE, seq)
    tk = tq // 2
    assert seq % tm == 0 and seq % tq == 0 and tk % 8 == 0 and d_sb % LANES == 0

    row = lambda a: a.reshape(depth, 1, a.shape[-1]).astype(F32)
    bf = lambda a: a.astype(BF16)
    g1, gm, g2, gp = row(ffn1_norm), row(mix_norm), row(ffn2_norm), row(ple_norm)
    qg = row(jnp.tile(q_norm, (1, n_heads)))
    kg = row(jnp.tile(k_norm, (1, n_heads)))
    cb, lam, ps = row(conv_b), row(lru_lambda), row(pool_scale)
    bgate = row(jnp.concatenate([lru_ba, lru_bx], axis=-1))
    wgate = bf(jnp.concatenate([_block_diag(lru_wa), _block_diag(lru_wx)], axis=-1))
    poolw = bf(_block_diag(pool_w))
    f1w1, f1w3, f1w2 = bf(ffn1_w1), bf(ffn1_w3), bf(ffn1_w2)
    f2w1, f2w3, f2w2 = bf(ffn2_w1), bf(ffn2_w3), bf(ffn2_w2)
    win, wo, wg, wp = bf(w_in), bf(w_out), bf(ple_gate_w), bf(ple_proj)
    convw = conv_w.astype(F32)
    head_mean = bf(jnp.kron(jnp.eye(n_heads, dtype=F32), jnp.full((HEAD_DIM, HEAD_DIM), 1.0 / HEAD_DIM, F32)))
    pool_win = jnp.repeat(jnp.asarray(POOL_WINDOWS, F32), d_pool // len(POOL_WINDOWS)).reshape(1, d_pool)
    idx = jnp.arange(tk)
    tri = (idx[:, None] >= idx[None, :]).astype(BF16)
    attn_rows = (LANES // HEAD_DIM) * tq
    score_bound = (HEAD_DIM ** 0.5 * LOG2E) * jnp.max(jnp.abs(q_norm), axis=-1) * jnp.max(jnp.abs(k_norm), axis=-1)
    attn_limit = (UNDERFLOW_BITS + score_bound * 2.0 ** -8).astype(F32)

    tok_grid = (bsz, seq // tm)
    tok = lambda width: pl.BlockSpec((1, tm, width), lambda b, s: (b, s, 0))
    seq_params = pltpu.CompilerParams(dimension_semantics=("arbitrary", "arbitrary"),
                                      vmem_limit_bytes=VMEM_LIMIT)

    for layer in range(depth):
        lspec = functools.partial(_layer_spec, layer=layer)
        x1, q, k, v, mix = pl.pallas_call(
            functools.partial(_mix_in_kernel, d_sb=d_sb, d_lru=d_lru, d_pool=d_pool),
            grid=tok_grid,
            in_specs=[tok(d_model), lspec(g1.shape), lspec(f1w1.shape), lspec(f1w3.shape), lspec(f1w2.shape),
                      lspec(gm.shape), lspec(win.shape), lspec(qg.shape), lspec(kg.shape),
                      _const_spec(head_mean.shape), lspec(convw.shape), lspec(cb.shape), lspec(wgate.shape),
                      lspec(bgate.shape), lspec(lam.shape), lspec(poolw.shape), lspec(ps.shape),
                      _const_spec(pool_win.shape)],
            out_specs=[tok(d_model), tok(d_sb), tok(d_sb), tok(d_sb), tok(d_mix)],
            out_shape=[jax.ShapeDtypeStruct((bsz, seq, d_model), F32),
                       jax.ShapeDtypeStruct((bsz, seq, d_sb), BF16),
                       jax.ShapeDtypeStruct((bsz, seq, d_sb), BF16),
                       jax.ShapeDtypeStruct((bsz, seq, d_sb), BF16),
                       jax.ShapeDtypeStruct((bsz, seq, d_mix), BF16)],
            scratch_shapes=[pltpu.VMEM((CONV_TAIL, d_lru), F32), pltpu.VMEM((POOL_TAIL, d_pool), F32),
                            pltpu.VMEM((1, d_lru), F32)],
            compiler_params=seq_params,
            name=f"mix_in_{layer}",
        )(x, g1, f1w1, f1w3, f1w2, gm, win, qg, kg, head_mean, convw, cb, wgate, bgate, lam, poolw, ps,
          pool_win)

        osb = pl.pallas_call(
            functools.partial(_attn_kernel, layer=layer),
            grid=(bsz, d_sb // LANES, seq // tq),
            in_specs=[pl.BlockSpec((1, tq, LANES), lambda b, hp, i: (b, i, hp)),
                      pl.BlockSpec((1, seq, LANES), lambda b, hp, i: (b, 0, hp)),
                      pl.BlockSpec((1, seq, LANES), lambda b, hp, i: (b, 0, hp)),
                      pl.BlockSpec((tk, tk), lambda b, hp, i: (0, 0)),
                      pl.BlockSpec(memory_space=pltpu.SMEM)],
            out_specs=pl.BlockSpec((1, tq, LANES), lambda b, hp, i: (b, i, hp)),
            out_shape=jax.ShapeDtypeStruct((bsz, seq, d_sb), BF16),
            scratch_shapes=[pltpu.VMEM((attn_rows, LANES), F32), pltpu.VMEM((attn_rows, 1), F32),
                            pltpu.VMEM((attn_rows, tk), BF16), pltpu.VMEM((attn_rows, tk), BF16)]
                           + [pltpu.VMEM((attn_rows, tk), F32)] * 4,
            compiler_params=pltpu.CompilerParams(
                dimension_semantics=("arbitrary", "arbitrary", "arbitrary"), vmem_limit_bytes=VMEM_LIMIT),
            name=f"attn_{layer}",
        )(q, k, v, tri, attn_limit)

        x = pl.pallas_call(
            _mix_out_kernel,
            grid=tok_grid,
            in_specs=[tok(d_model), tok(d_sb), tok(d_mix),
                      pl.BlockSpec((None, 1, tm, ple_dim), lambda b, s, layer=layer: (layer, b, s, 0)),
                      lspec(wo.shape), lspec(g2.shape), lspec(f2w1.shape), lspec(f2w3.shape),
                      lspec(f2w2.shape), lspec(gp.shape), lspec(wg.shape), lspec(wp.shape)],
            out_specs=tok(d_model),
            out_shape=jax.ShapeDtypeStruct((bsz, seq, d_model), F32),
            compiler_params=seq_params,
            name=f"mix_out_{layer}",
        )(x1, osb, mix, p, wo, g2, f2w1, f2w3, f2w2, gp, wg, wp)
    return x
```

```python
import functools

import jax
import jax.numpy as jnp
from jax import lax
from jax.experimental import pallas as pl
from jax.experimental.pallas import tpu as pltpu

EPS = 1e-6
HEAD_DIM = 64
POOL_WINDOWS = (2, 4, 8, 16)
CONV_WIDTH = 4
LRU_C = 8.0

LANES = 128
MXU_TILE = 256
TOKEN_TILE = 512
ATTN_Q_TILE = 512
ATTN_Z_SLOTS = 4
LOG2E = 1.4426950408889634
UNDERFLOW_BITS = 1088.0
SOFTPLUS_CLAMP = 115.0
FFN_CHUNK = 6 * MXU_TILE
CONV_TAIL = 8
POOL_TAIL = 16
VMEM_LIMIT = 60 * 1024 * 1024

F32 = jnp.float32
BF16 = jnp.bfloat16


def _dot(a, b):
    return jnp.dot(a, b, preferred_element_type=F32)


def _rms(x, g):
    ms = jnp.mean(x * x, axis=-1, keepdims=True)
    return x * lax.rsqrt(ms + EPS) * g


def _ffn_half_step(x, g_ref, w1_ref, w3_ref, w2_ref):
    h = _rms(x, g_ref[...]).astype(BF16)
    d_ff = w1_ref.shape[1]
    acc = None
    for c0 in range(0, d_ff, FFN_CHUNK):
        c1 = min(c0 + FFN_CHUNK, d_ff)
        a = _dot(h, w1_ref[:, c0:c1])
        b = _dot(h, w3_ref[:, c0:c1])
        m = (a * jax.nn.sigmoid(a) * b).astype(BF16)
        part = _dot(m, w2_ref[c0:c1, :])
        acc = part if acc is None else acc + part
    return x + 0.5 * acc


def _head_rms(q, head_mean_ref, g):
    sq = (q * q).astype(BF16)
    width = q.shape[1]
    parts = []
    for c0 in range(0, width, MXU_TILE):
        c1 = min(c0 + MXU_TILE, width)
        parts.append(_dot(sq[:, c0:c1], head_mean_ref[c0:c1, c0:c1]))
    ms = jnp.concatenate(parts, axis=1)
    return q * lax.rsqrt(ms + EPS) * g


def _shift_rows(ext, shift, tail):
    if shift == 0:
        return ext[tail:]
    return pltpu.roll(ext, shift, axis=0)[tail:]


def _neg_expm1(y):
    poly = 1.0 / 40320.0
    for k in (5040.0, 720.0, 120.0, 24.0, 6.0, 2.0, 1.0):
        poly = poly * y + 1.0 / k
    series = -y * poly
    return jnp.where(y > -0.25, series, 1.0 - jnp.exp(y))


def _linear_scan(a, u, h0):
    rows = a.shape[0]
    row = lax.broadcasted_iota(jnp.int32, a.shape, 0)
    d = 1
    while d < rows:
        keep = row >= d
        a_prev = jnp.where(keep, pltpu.roll(a, d, axis=0), 1.0)
        u_prev = jnp.where(keep, pltpu.roll(u, d, axis=0), 0.0)
        u = a * u_prev + u
        a = a * a_prev
        d *= 2
    return u + a * h0


def _mix_in_kernel(x_ref, g1_ref, w1_ref, w3_ref, w2_ref, gm_ref, win_ref, qg_ref, kg_ref, hmean_ref,
                   convw_ref, convb_ref, wgate_ref, bgate_ref, lam_ref, poolw_ref, pscale_ref, pwin_ref,
                   x1_ref, q_ref, k_ref, v_ref, mix_ref,
                   xb_tail, xc_tail, h_carry, *, d_sb, d_lru, d_pool):
    s = pl.program_id(1)
    tm = x_ref.shape[1]

    @pl.when(s == 0)
    def _():
        xb_tail[...] = jnp.zeros_like(xb_tail)
        xc_tail[...] = jnp.zeros_like(xc_tail)
        h_carry[...] = jnp.zeros_like(h_carry)

    x1 = _ffn_half_step(x_ref[0], g1_ref, w1_ref, w3_ref, w2_ref)
    x1_ref[0] = x1

    h = _rms(x1, gm_ref[...]).astype(BF16)
    proj = _dot(h, win_ref[...])
    o = 0
    q = proj[:, o:o + d_sb]; o += d_sb
    k = proj[:, o:o + d_sb]; o += d_sb
    v = proj[:, o:o + d_sb]; o += d_sb
    xb = proj[:, o:o + d_lru]; o += d_lru
    gb = proj[:, o:o + d_lru]; o += d_lru
    xc = proj[:, o:o + d_pool]

    q_ref[0] = (_head_rms(q, hmean_ref, qg_ref[...]) * (HEAD_DIM ** -0.5 * LOG2E)).astype(BF16)
    k_ref[0] = _head_rms(k, hmean_ref, kg_ref[...]).astype(BF16)
    v_ref[0] = v.astype(BF16)

    ext = jnp.concatenate([xb_tail[...], xb], axis=0)
    conv = convb_ref[...] + convw_ref[CONV_WIDTH - 1:CONV_WIDTH, :] * xb
    for j in range(1, CONV_WIDTH):
        conv = conv + convw_ref[CONV_WIDTH - 1 - j:CONV_WIDTH - j, :] * _shift_rows(ext, j, CONV_TAIL)
    xb_tail[...] = xb[tm - CONV_TAIL:, :]
    gates = jax.nn.sigmoid(_dot(conv.astype(BF16), wgate_ref[...]) + bgate_ref[...])
    r_gate = gates[:, :d_lru]
    i_gate = gates[:, d_lru:]
    lam = lam_ref[...]
    softplus_neg_lam = jnp.maximum(-lam, 0.0) + jnp.log1p(jnp.exp(-jnp.abs(lam)))
    log_a = (-LRU_C * r_gate) * softplus_neg_lam
    a = jnp.exp(log_a)
    u = jnp.sqrt(_neg_expm1(2.0 * log_a)) * (i_gate * conv)
    hseq = _linear_scan(a, u, h_carry[...])
    h_carry[...] = hseq[tm - 1:tm, :]
    mix_ref[0, :, 0:d_lru] = (jax.nn.gelu(gb) * hseq).astype(BF16)

    ext = jnp.concatenate([xc_tail[...], xc], axis=0)
    lane = lax.broadcasted_iota(jnp.int32, ext.shape, 1)
    group = d_pool // len(POOL_WINDOWS)
    run = ext
    wsum = None
    for gi, w in enumerate(POOL_WINDOWS):
        run = run + pltpu.roll(run, w // 2, axis=0)
        wsum = run if wsum is None else jnp.where(lane >= gi * group, run, wsum)
    wsum = wsum[POOL_TAIL:]
    xc_tail[...] = xc[tm - POOL_TAIL:, :]
    pos = (s * tm + 1 + lax.broadcasted_iota(jnp.int32, xc.shape, 0)).astype(F32)
    count = jnp.minimum(pos, pwin_ref[...])
    pooled = wsum / count - xc
    mix_ref[0, :, d_lru:d_lru + d_pool] = (_dot(pooled.astype(BF16), poolw_ref[...]) * pscale_ref[...]).astype(BF16)


def _attn_kernel(q_ref, k_ref, v_ref, tri_ref, limit_ref, o_ref, acc_ref, carry_ref, wa_ref, wb_ref, *z_refs,
                 layer):
    i = pl.program_id(2)
    tq = q_ref.shape[1]
    tk = tri_ref.shape[1]
    heads = LANES // HEAD_DIM
    rows = heads * tq
    half = rows // 2
    lane = lax.broadcasted_iota(jnp.int32, (tk, LANES), 1)
    qs = jnp.concatenate(
        [jnp.where((lane >= hd * HEAD_DIM) & (lane < (hd + 1) * HEAD_DIM), q_ref[0, part * tk:(part + 1) * tk, :],
                   jnp.zeros((tk, LANES), q_ref.dtype))
         for part in range(tq // tk) for hd in range(heads)], axis=0)
    tri = tri_ref[...]
    acc_ref[...] = jnp.zeros_like(acc_ref)
    carry_ref[...] = jnp.zeros_like(carry_ref)

    def keys(ref, j):
        return ref[0, pl.ds(pl.multiple_of(j * tk, tk), tk), :]

    def scores(j, z_ref, lo=0):
        z_ref[lo:, :] = lax.dot_general(qs[lo:], keys(k_ref, j), (((1,), (1,)), ((), ())),
                                        preferred_element_type=F32)

    def weights(z_ref, w_ref, diagonal_offset=None, lo=0):
        z = z_ref[lo:, :]
        sp = jnp.maximum(z, jnp.log(1.0 + jnp.exp2(jnp.minimum(z, SOFTPLUS_CLAMP))) * LOG2E)
        if diagonal_offset is not None:
            r = lo + lax.broadcasted_iota(jnp.int32, (rows - lo, tk), 0)
            qpos = jnp.where(r >= half, tk, 0) + lax.rem(r, tk)
            kpos = diagonal_offset + lax.broadcasted_iota(jnp.int32, (rows - lo, tk), 1)
            strictly_before = kpos < qpos
            sp = jnp.where(strictly_before, sp, 0.0)
        c = _dot(sp.astype(BF16), tri)
        carry = carry_ref[lo:, :]
        w = jnp.exp2(z_ref[lo:, :] - c - carry)
        if diagonal_offset is not None:
            w = jnp.where(strictly_before, w, 0.0)
        w_ref[lo:, :] = w.astype(BF16)
        carry_ref[lo:, :] = carry + c[:, 0:1]

    def accumulate(j, w_ref, lo=0):
        acc_ref[lo:, :] += _dot(w_ref[lo:, :], keys(v_ref, j))

    top = 2 * i + 1
    w_refs = (wa_ref, wb_ref)
    n_z = len(z_refs)
    scores(top, z_refs[2], lo=half)
    scores(top - 1, z_refs[3])
    weights(z_refs[2], wa_ref, diagonal_offset=tk, lo=half)
    scores(jnp.maximum(top - 2, 0), z_refs[0])
    accumulate(top, wa_ref, lo=half)
    weights(z_refs[3], wb_ref, diagonal_offset=0)
    scores(jnp.maximum(top - 3, 0), z_refs[1])

    def run(ja, count, feed_next):
        for s in range(count):
            accumulate(ja - s + 1, w_refs[(s + 1) % 2])
            if feed_next or s + 2 < count:
                scores(jnp.maximum(ja - s - 2, 0), z_refs[(s + 2) % n_z])
            weights(z_refs[s % n_z], w_refs[s % 2])

    rounds = (2 * i) // n_z
    limit = limit_ref[layer]

    def more(state):
        u, live = state
        return jnp.logical_and(u < rounds, live > 0)

    def body(state):
        u, _ = state
        run(top - 2 - n_z * u, n_z, True)
        return u + 1, (jnp.min(carry_ref[...]) < limit).astype(jnp.int32)

    done_rounds, live = lax.while_loop(more, body, (jnp.int32(0), jnp.int32(1)))
    left = jnp.where(live > 0, 2 * i - n_z * rounds, 0)
    for count in range(2, n_z, 2):
        @pl.when(left == count)
        def _(count=count):
            run(top - 2 - n_z * rounds, count, False)

    accumulate(top - 1 - n_z * done_rounds - left, wb_ref)
    for part in range(tq // tk):
        base = part * half
        out = acc_ref[base:base + tk, :]
        for hd in range(1, heads):
            out = jnp.where(lane >= hd * HEAD_DIM, acc_ref[base + hd * tk:base + (hd + 1) * tk, :], out)
        o_ref[0, part * tk:(part + 1) * tk, :] = out.astype(o_ref.dtype)


def _mix_out_kernel(x1_ref, osb_ref, mix_ref, p_ref, wo_ref, g2_ref, w1_ref, w3_ref, w2_ref,
                    gp_ref, wg_ref, wp_ref, out_ref):
    mixed = jnp.concatenate([osb_ref[0], mix_ref[0]], axis=1)
    x = x1_ref[0] + _dot(mixed, wo_ref[...])
    x = _ffn_half_step(x, g2_ref, w1_ref, w3_ref, w2_ref)
    gate = jax.nn.sigmoid(_dot(_rms(x, gp_ref[...]).astype(BF16), wg_ref[...]))
    emb = _dot(p_ref[0].astype(BF16), wp_ref[...])
    out_ref[0] = x + gate * emb


def _block_diag(w):
    l, h, i, j = w.shape
    eye = jnp.eye(h, dtype=w.dtype)
    return jnp.einsum('lhij,hg->lhigj', w, eye).reshape(l, h * i, h * j)


def _layer_spec(shape, layer):
    nd = len(shape) - 1
    return pl.BlockSpec((None,) + tuple(shape[1:]), lambda b, s: (layer,) + (0,) * nd,
                        pipeline_mode=pl.Buffered(1))


def _const_spec(shape):
    nd = len(shape)
    return pl.BlockSpec(tuple(shape), lambda *_: (0,) * nd, pipeline_mode=pl.Buffered(1))


def kernel(x, p, ffn1_norm, ffn1_w1, ffn1_w3, ffn1_w2, mix_norm, w_in, q_norm, k_norm, conv_w, conv_b,
           lru_wa, lru_ba, lru_wx, lru_bx, lru_lambda, pool_w, pool_scale, w_out, ffn2_norm, ffn2_w1,
           ffn2_w3, ffn2_w2, ple_norm, ple_gate_w, ple_proj):
    bsz, seq, d_model = x.shape
    depth = p.shape[0]
    ple_dim = p.shape[-1]
    d_lru = conv_b.shape[-1]
    d_pool = pool_scale.shape[-1]
    d_sb = (w_in.shape[-1] - 2 * d_lru - d_pool) // 3
    n_heads = d_sb // HEAD_DIM
    d_mix = d_lru + d_pool
    tm = min(TOKEN_TILE, seq)
    tq = min(ATTN_Q_TILE, seq)
    tk = tq // 2
    assert seq % tm == 0 and seq % tq == 0 and tk % 8 == 0 and d_sb % LANES == 0

    row = lambda a: a.reshape(depth, 1, a.shape[-1]).astype(F32)
    bf = lambda a: a.astype(BF16)
    g1, gm, g2, gp = row(ffn1_norm), row(mix_norm), row(ffn2_norm), row(ple_norm)
    qg = row(jnp.tile(q_norm, (1, n_heads)))
    kg = row(jnp.tile(k_norm, (1, n_heads)))
    cb, lam, ps = row(conv_b), row(lru_lambda), row(pool_scale)
    bgate = row(jnp.concatenate([lru_ba, lru_bx], axis=-1))
    wgate = bf(jnp.concatenate([_block_diag(lru_wa), _block_diag(lru_wx)], axis=-1))
    poolw = bf(_block_diag(pool_w))
    f1w1, f1w3, f1w2 = bf(ffn1_w1), bf(ffn1_w3), bf(ffn1_w2)
    f2w1, f2w3, f2w2 = bf(ffn2_w1), bf(ffn2_w3), bf(ffn2_w2)
    win, wo, wg, wp = bf(w_in), bf(w_out), bf(ple_gate_w), bf(ple_proj)
    convw = conv_w.astype(F32)
    head_mean = bf(jnp.kron(jnp.eye(n_heads, dtype=F32), jnp.full((HEAD_DIM, HEAD_DIM), 1.0 / HEAD_DIM, F32)))
    pool_win = jnp.repeat(jnp.asarray(POOL_WINDOWS, F32), d_pool // len(POOL_WINDOWS)).reshape(1, d_pool)
    idx = jnp.arange(tk)
    tri = (idx[:, None] >= idx[None, :]).astype(BF16)
    attn_rows = (LANES // HEAD_DIM) * tq
    score_bound = (HEAD_DIM ** 0.5 * LOG2E) * jnp.max(jnp.abs(q_norm), axis=-1) * jnp.max(jnp.abs(k_norm), axis=-1)
    attn_limit = (UNDERFLOW_BITS + score_bound * 2.0 ** -8).astype(F32)

    tok_grid = (bsz, seq // tm)
    tok = lambda width: pl.BlockSpec((1, tm, width), lambda b, s: (b, s, 0))
    seq_params = pltpu.CompilerParams(dimension_semantics=("arbitrary", "arbitrary"),
                                      vmem_limit_bytes=VMEM_LIMIT)

    for layer in range(depth):
        lspec = functools.partial(_layer_spec, layer=layer)
        x1, q, k, v, mix = pl.pallas_call(
            functools.partial(_mix_in_kernel, d_sb=d_sb, d_lru=d_lru, d_pool=d_pool),
            grid=tok_grid,
            in_specs=[tok(d_model), lspec(g1.shape), lspec(f1w1.shape), lspec(f1w3.shape), lspec(f1w2.shape),
                      lspec(gm.shape), lspec(win.shape), lspec(qg.shape), lspec(kg.shape),
                      _const_spec(head_mean.shape), lspec(convw.shape), lspec(cb.shape), lspec(wgate.shape),
                      lspec(bgate.shape), lspec(lam.shape), lspec(poolw.shape), lspec(ps.shape),
                      _const_spec(pool_win.shape)],
            out_specs=[tok(d_model), tok(d_sb), tok(d_sb), tok(d_sb), tok(d_mix)],
            out_shape=[jax.ShapeDtypeStruct((bsz, seq, d_model), F32),
                       jax.ShapeDtypeStruct((bsz, seq, d_sb), BF16),
                       jax.ShapeDtypeStruct((bsz, seq, d_sb), BF16),
                       jax.ShapeDtypeStruct((bsz, seq, d_sb), BF16),
                       jax.ShapeDtypeStruct((bsz, seq, d_mix), BF16)],
            scratch_shapes=[pltpu.VMEM((CONV_TAIL, d_lru), F32), pltpu.VMEM((POOL_TAIL, d_pool), F32),
                            pltpu.VMEM((1, d_lru), F32)],
            compiler_params=seq_params,
            name=f"mix_in_{layer}",
        )(x, g1, f1w1, f1w3, f1w2, gm, win, qg, kg, head_mean, convw, cb, wgate, bgate, lam, poolw, ps,
          pool_win)

        osb = pl.pallas_call(
            functools.partial(_attn_kernel, layer=layer),
            grid=(bsz, d_sb // LANES, seq // tq),
            in_specs=[pl.BlockSpec((1, tq, LANES), lambda b, hp, i: (b, i, hp)),
                      pl.BlockSpec((1, seq, LANES), lambda b, hp, i: (b, 0, hp)),
                      pl.BlockSpec((1, seq, LANES), lambda b, hp, i: (b, 0, hp)),
                      pl.BlockSpec((tk, tk), lambda b, hp, i: (0, 0)),
                      pl.BlockSpec(memory_space=pltpu.SMEM)],
            out_specs=pl.BlockSpec((1, tq, LANES), lambda b, hp, i: (b, i, hp)),
            out_shape=jax.ShapeDtypeStruct((bsz, seq, d_sb), BF16),
            scratch_shapes=[pltpu.VMEM((attn_rows, LANES), F32), pltpu.VMEM((attn_rows, 1), F32),
                            pltpu.VMEM((attn_rows, tk), BF16), pltpu.VMEM((attn_rows, tk), BF16)]
                           + [pltpu.VMEM((attn_rows, tk), F32)] * ATTN_Z_SLOTS,
            compiler_params=pltpu.CompilerParams(
                dimension_semantics=("arbitrary", "arbitrary", "arbitrary"), vmem_limit_bytes=VMEM_LIMIT),
            name=f"attn_{layer}",
        )(q, k, v, tri, attn_limit)

        x = pl.pallas_call(
            _mix_out_kernel,
            grid=tok_grid,
            in_specs=[tok(d_model), tok(d_sb), tok(d_mix),
                      pl.BlockSpec((None, 1, tm, ple_dim), lambda b, s, layer=layer: (layer, b, s, 0)),
                      lspec(wo.shape), lspec(g2.shape), lspec(f2w1.shape), lspec(f2w3.shape),
                      lspec(f2w2.shape), lspec(gp.shape), lspec(wg.shape), lspec(wp.shape)],
            out_specs=tok(d_model),
            out_shape=jax.ShapeDtypeStruct((bsz, seq, d_model), F32),
            compiler_params=seq_params,
            name=f"mix_out_{layer}",
        )(x1, osb, mix, p, wo, g2, f2w1, f2w3, f2w2, gp, wg, wp)
    return x
```

```python
import functools

import jax
import jax.numpy as jnp
from jax import lax
from jax.experimental import pallas as pl
from jax.experimental.pallas import tpu as pltpu

EPS = 1e-6
HEAD_DIM = 64
POOL_WINDOWS = (2, 4, 8, 16)
CONV_WIDTH = 4
LRU_C = 8.0

LANES = 128
MXU_TILE = 256
TOKEN_TILE = 512
ATTN_Q_TILE = 512
ATTN_Z_SLOTS = 6
FIRST_LEG = 4
LOG2E = 1.4426950408889634
UNDERFLOW_BITS = 1088.0
SOFTPLUS_CLAMP = 115.0
FFN_CHUNK = 6 * MXU_TILE
CONV_TAIL = 8
POOL_TAIL = 16
VMEM_LIMIT = 60 * 1024 * 1024

F32 = jnp.float32
BF16 = jnp.bfloat16


def _dot(a, b):
    return jnp.dot(a, b, preferred_element_type=F32)


def _rms(x, g):
    ms = jnp.mean(x * x, axis=-1, keepdims=True)
    return x * lax.rsqrt(ms + EPS) * g


def _ffn_half_step(x, g_ref, w1_ref, w3_ref, w2_ref):
    h = _rms(x, g_ref[...]).astype(BF16)
    d_ff = w1_ref.shape[1]
    acc = None
    for c0 in range(0, d_ff, FFN_CHUNK):
        c1 = min(c0 + FFN_CHUNK, d_ff)
        a = _dot(h, w1_ref[:, c0:c1])
        b = _dot(h, w3_ref[:, c0:c1])
        m = (a * jax.nn.sigmoid(a) * b).astype(BF16)
        part = _dot(m, w2_ref[c0:c1, :])
        acc = part if acc is None else acc + part
    return x + 0.5 * acc


def _head_rms(q, head_mean_ref, g):
    sq = (q * q).astype(BF16)
    width = q.shape[1]
    parts = []
    for c0 in range(0, width, MXU_TILE):
        c1 = min(c0 + MXU_TILE, width)
        parts.append(_dot(sq[:, c0:c1], head_mean_ref[c0:c1, c0:c1]))
    ms = jnp.concatenate(parts, axis=1)
    return q * lax.rsqrt(ms + EPS) * g


def _shift_rows(ext, shift, tail):
    if shift == 0:
        return ext[tail:]
    return pltpu.roll(ext, shift, axis=0)[tail:]


def _neg_expm1(y):
    poly = 1.0 / 40320.0
    for k in (5040.0, 720.0, 120.0, 24.0, 6.0, 2.0, 1.0):
        poly = poly * y + 1.0 / k
    series = -y * poly
    return jnp.where(y > -0.25, series, 1.0 - jnp.exp(y))


def _linear_scan(a, u, h0):
    rows = a.shape[0]
    row = lax.broadcasted_iota(jnp.int32, a.shape, 0)
    d = 1
    while d < rows:
        keep = row >= d
        a_prev = jnp.where(keep, pltpu.roll(a, d, axis=0), 1.0)
        u_prev = jnp.where(keep, pltpu.roll(u, d, axis=0), 0.0)
        u = a * u_prev + u
        a = a * a_prev
        d *= 2
    return u + a * h0


def _mix_in_kernel(x_ref, g1_ref, w1_ref, w3_ref, w2_ref, gm_ref, win_ref, qg_ref, kg_ref, hmean_ref,
                   convw_ref, convb_ref, wgate_ref, bgate_ref, lam_ref, poolw_ref, pscale_ref, pwin_ref,
                   x1_ref, q_ref, k_ref, v_ref, mix_ref,
                   xb_tail, xc_tail, h_carry, *, d_sb, d_lru, d_pool):
    s = pl.program_id(1)
    tm = x_ref.shape[1]

    @pl.when(s == 0)
    def _():
        xb_tail[...] = jnp.zeros_like(xb_tail)
        xc_tail[...] = jnp.zeros_like(xc_tail)
        h_carry[...] = jnp.zeros_like(h_carry)

    x1 = _ffn_half_step(x_ref[0], g1_ref, w1_ref, w3_ref, w2_ref)
    x1_ref[0] = x1

    h = _rms(x1, gm_ref[...]).astype(BF16)
    proj = _dot(h, win_ref[...])
    o = 0
    q = proj[:, o:o + d_sb]; o += d_sb
    k = proj[:, o:o + d_sb]; o += d_sb
    v = proj[:, o:o + d_sb]; o += d_sb
    xb = proj[:, o:o + d_lru]; o += d_lru
    gb = proj[:, o:o + d_lru]; o += d_lru
    xc = proj[:, o:o + d_pool]

    q_ref[0] = (_head_rms(q, hmean_ref, qg_ref[...]) * (HEAD_DIM ** -0.5 * LOG2E)).astype(BF16)
    k_ref[0] = _head_rms(k, hmean_ref, kg_ref[...]).astype(BF16)
    v_ref[0] = v.astype(BF16)

    ext = jnp.concatenate([xb_tail[...], xb], axis=0)
    conv = convb_ref[...] + convw_ref[CONV_WIDTH - 1:CONV_WIDTH, :] * xb
    for j in range(1, CONV_WIDTH):
        conv = conv + convw_ref[CONV_WIDTH - 1 - j:CONV_WIDTH - j, :] * _shift_rows(ext, j, CONV_TAIL)
    xb_tail[...] = xb[tm - CONV_TAIL:, :]
    gates = jax.nn.sigmoid(_dot(conv.astype(BF16), wgate_ref[...]) + bgate_ref[...])
    r_gate = gates[:, :d_lru]
    i_gate = gates[:, d_lru:]
    lam = lam_ref[...]
    softplus_neg_lam = jnp.maximum(-lam, 0.0) + jnp.log1p(jnp.exp(-jnp.abs(lam)))
    log_a = (-LRU_C * r_gate) * softplus_neg_lam
    a = jnp.exp(log_a)
    u = jnp.sqrt(_neg_expm1(2.0 * log_a)) * (i_gate * conv)
    hseq = _linear_scan(a, u, h_carry[...])
    h_carry[...] = hseq[tm - 1:tm, :]
    mix_ref[0, :, 0:d_lru] = (jax.nn.gelu(gb) * hseq).astype(BF16)

    ext = jnp.concatenate([xc_tail[...], xc], axis=0)
    lane = lax.broadcasted_iota(jnp.int32, ext.shape, 1)
    group = d_pool // len(POOL_WINDOWS)
    run = ext
    wsum = None
    for gi, w in enumerate(POOL_WINDOWS):
        run = run + pltpu.roll(run, w // 2, axis=0)
        wsum = run if wsum is None else jnp.where(lane >= gi * group, run, wsum)
    wsum = wsum[POOL_TAIL:]
    xc_tail[...] = xc[tm - POOL_TAIL:, :]
    pos = (s * tm + 1 + lax.broadcasted_iota(jnp.int32, xc.shape, 0)).astype(F32)
    count = jnp.minimum(pos, pwin_ref[...])
    pooled = wsum / count - xc
    mix_ref[0, :, d_lru:d_lru + d_pool] = (_dot(pooled.astype(BF16), poolw_ref[...]) * pscale_ref[...]).astype(BF16)


def _attn_kernel(q_ref, k_ref, v_ref, tri_ref, limit_ref, o_ref, acc_ref, carry_ref, wa_ref, wb_ref, *z_refs,
                 layer):
    i = pl.program_id(2)
    tq = q_ref.shape[1]
    tk = tri_ref.shape[1]
    heads = LANES // HEAD_DIM
    rows = heads * tq
    half = rows // 2
    lane = lax.broadcasted_iota(jnp.int32, (tk, LANES), 1)
    qs = jnp.concatenate(
        [jnp.where((lane >= hd * HEAD_DIM) & (lane < (hd + 1) * HEAD_DIM), q_ref[0, part * tk:(part + 1) * tk, :],
                   jnp.zeros((tk, LANES), q_ref.dtype))
         for part in range(tq // tk) for hd in range(heads)], axis=0)
    tri = tri_ref[...]
    acc_ref[...] = jnp.zeros_like(acc_ref)
    carry_ref[...] = jnp.zeros_like(carry_ref)

    def keys(ref, j):
        return ref[0, pl.ds(pl.multiple_of(j * tk, tk), tk), :]

    def scores(j, z_ref, lo=0):
        z_ref[lo:, :] = lax.dot_general(qs[lo:], keys(k_ref, j), (((1,), (1,)), ((), ())),
                                        preferred_element_type=F32)

    def weights(z_ref, w_ref, diagonal_offset=None, lo=0):
        z = z_ref[lo:, :]
        sp = jnp.maximum(z, jnp.log(1.0 + jnp.exp2(jnp.minimum(z, SOFTPLUS_CLAMP))) * LOG2E)
        if diagonal_offset is not None:
            r = lo + lax.broadcasted_iota(jnp.int32, (rows - lo, tk), 0)
            qpos = jnp.where(r >= half, tk, 0) + lax.rem(r, tk)
            kpos = diagonal_offset + lax.broadcasted_iota(jnp.int32, (rows - lo, tk), 1)
            strictly_before = kpos < qpos
            sp = jnp.where(strictly_before, sp, 0.0)
        c = _dot(sp.astype(BF16), tri)
        carry = carry_ref[lo:, :]
        w = jnp.exp2(z_ref[lo:, :] - c - carry)
        if diagonal_offset is not None:
            w = jnp.where(strictly_before, w, 0.0)
        w_ref[lo:, :] = w.astype(BF16)
        carry_ref[lo:, :] = carry + c[:, 0:1]

    def accumulate(j, w_ref, lo=0):
        acc_ref[lo:, :] += _dot(w_ref[lo:, :], keys(v_ref, j))

    top = 2 * i + 1
    w_refs = (wa_ref, wb_ref)
    n_z = len(z_refs)
    scores(top, z_refs[2], lo=half)
    scores(top - 1, z_refs[3])
    weights(z_refs[2], wa_ref, diagonal_offset=tk, lo=half)
    scores(jnp.maximum(top - 2, 0), z_refs[0])
    accumulate(top, wa_ref, lo=half)
    weights(z_refs[3], wb_ref, diagonal_offset=0)
    scores(jnp.maximum(top - 3, 0), z_refs[1])

    def run(ja, count, feed_next, base=0):
        for s in range(count):
            accumulate(ja - s + 1, w_refs[(s + 1) % 2])
            if feed_next or s + 2 < count:
                scores(jnp.maximum(ja - s - 2, 0), z_refs[(base + s + 2) % n_z])
            weights(z_refs[(base + s) % n_z], w_refs[s % 2])

    rounds = (2 * i) // n_z
    limit = limit_ref[layer]

    def unfinished():
        return jnp.min(carry_ref[...]) < limit

    def more(state):
        u, live, _ = state
        return jnp.logical_and(u < rounds, live > 0)

    def body(state):
        u, _, visited = state
        ja = top - 2 - n_z * u
        run(ja, FIRST_LEG, True)
        go_on = unfinished()

        @pl.when(go_on)
        def _():
            run(ja - FIRST_LEG, n_z - FIRST_LEG, True, base=FIRST_LEG)

        live = jnp.logical_and(go_on, unfinished()).astype(jnp.int32)
        return u + 1, live, visited + jnp.where(go_on, n_z, FIRST_LEG)

    _, live, visited = lax.while_loop(more, body, (jnp.int32(0), jnp.int32(1), jnp.int32(0)))
    left = jnp.where(live > 0, 2 * i - n_z * rounds, 0)
    for count in range(2, n_z, 2):
        @pl.when(left == count)
        def _(count=count):
            run(top - 2 - n_z * rounds, count, False)

    accumulate(top - 1 - visited - left, wb_ref)
    for part in range(tq // tk):
        base = part * half
        out = acc_ref[base:base + tk, :]
        for hd in range(1, heads):
            out = jnp.where(lane >= hd * HEAD_DIM, acc_ref[base + hd * tk:base + (hd + 1) * tk, :], out)
        o_ref[0, part * tk:(part + 1) * tk, :] = out.astype(o_ref.dtype)


def _mix_out_kernel(x1_ref, osb_ref, mix_ref, p_ref, wo_ref, g2_ref, w1_ref, w3_ref, w2_ref,
                    gp_ref, wg_ref, wp_ref, out_ref):
    mixed = jnp.concatenate([osb_ref[0], mix_ref[0]], axis=1)
    x = x1_ref[0] + _dot(mixed, wo_ref[...])
    x = _ffn_half_step(x, g2_ref, w1_ref, w3_ref, w2_ref)
    gate = jax.nn.sigmoid(_dot(_rms(x, gp_ref[...]).astype(BF16), wg_ref[...]))
    emb = _dot(p_ref[0].astype(BF16), wp_ref[...])
    out_ref[0] = x + gate * emb


def _block_diag(w):
    l, h, i, j = w.shape
    eye = jnp.eye(h, dtype=w.dtype)
    return jnp.einsum('lhij,hg->lhigj', w, eye).reshape(l, h * i, h * j)


def _layer_spec(shape, layer):
    nd = len(shape) - 1
    return pl.BlockSpec((None,) + tuple(shape[1:]), lambda b, s: (layer,) + (0,) * nd,
                        pipeline_mode=pl.Buffered(1))


def _const_spec(shape):
    nd = len(shape)
    return pl.BlockSpec(tuple(shape), lambda *_: (0,) * nd, pipeline_mode=pl.Buffered(1))


def kernel(x, p, ffn1_norm, ffn1_w1, ffn1_w3, ffn1_w2, mix_norm, w_in, q_norm, k_norm, conv_w, conv_b,
           lru_wa, lru_ba, lru_wx, lru_bx, lru_lambda, pool_w, pool_scale, w_out, ffn2_norm, ffn2_w1,
           ffn2_w3, ffn2_w2, ple_norm, ple_gate_w, ple_proj):
    bsz, seq, d_model = x.shape
    depth = p.shape[0]
    ple_dim = p.shape[-1]
    d_lru = conv_b.shape[-1]
    d_pool = pool_scale.shape[-1]
    d_sb = (w_in.shape[-1] - 2 * d_lru - d_pool) // 3
    n_heads = d_sb // HEAD_DIM
    d_mix = d_lru + d_pool
    tm = min(TOKEN_TILE, seq)
    tq = min(ATTN_Q_TILE, seq)
    tk = tq // 2
    assert seq % tm == 0 and seq % tq == 0 and tk % 8 == 0 and d_sb % LANES == 0

    row = lambda a: a.reshape(depth, 1, a.shape[-1]).astype(F32)
    bf = lambda a: a.astype(BF16)
    g1, gm, g2, gp = row(ffn1_norm), row(mix_norm), row(ffn2_norm), row(ple_norm)
    qg = row(jnp.tile(q_norm, (1, n_heads)))
    kg = row(jnp.tile(k_norm, (1, n_heads)))
    cb, lam, ps = row(conv_b), row(lru_lambda), row(pool_scale)
    bgate = row(jnp.concatenate([lru_ba, lru_bx], axis=-1))
    wgate = bf(jnp.concatenate([_block_diag(lru_wa), _block_diag(lru_wx)], axis=-1))
    poolw = bf(_block_diag(pool_w))
    f1w1, f1w3, f1w2 = bf(ffn1_w1), bf(ffn1_w3), bf(ffn1_w2)
    f2w1, f2w3, f2w2 = bf(ffn2_w1), bf(ffn2_w3), bf(ffn2_w2)
    win, wo, wg, wp = bf(w_in), bf(w_out), bf(ple_gate_w), bf(ple_proj)
    convw = conv_w.astype(F32)
    head_mean = bf(jnp.kron(jnp.eye(n_heads, dtype=F32), jnp.full((HEAD_DIM, HEAD_DIM), 1.0 / HEAD_DIM, F32)))
    pool_win = jnp.repeat(jnp.asarray(POOL_WINDOWS, F32), d_pool // len(POOL_WINDOWS)).reshape(1, d_pool)
    idx = jnp.arange(tk)
    tri = (idx[:, None] >= idx[None, :]).astype(BF16)
    attn_rows = (LANES // HEAD_DIM) * tq
    score_bound = (HEAD_DIM ** 0.5 * LOG2E) * jnp.max(jnp.abs(q_norm), axis=-1) * jnp.max(jnp.abs(k_norm), axis=-1)
    attn_limit = (UNDERFLOW_BITS + score_bound * 2.0 ** -8).astype(F32)

    tok_grid = (bsz, seq // tm)
    tok = lambda width: pl.BlockSpec((1, tm, width), lambda b, s: (b, s, 0))
    seq_params = pltpu.CompilerParams(dimension_semantics=("arbitrary", "arbitrary"),
                                      vmem_limit_bytes=VMEM_LIMIT)

    for layer in range(depth):
        lspec = functools.partial(_layer_spec, layer=layer)
        x1, q, k, v, mix = pl.pallas_call(
            functools.partial(_mix_in_kernel, d_sb=d_sb, d_lru=d_lru, d_pool=d_pool),
            grid=tok_grid,
            in_specs=[tok(d_model), lspec(g1.shape), lspec(f1w1.shape), lspec(f1w3.shape), lspec(f1w2.shape),
                      lspec(gm.shape), lspec(win.shape), lspec(qg.shape), lspec(kg.shape),
                      _const_spec(head_mean.shape), lspec(convw.shape), lspec(cb.shape), lspec(wgate.shape),
                      lspec(bgate.shape), lspec(lam.shape), lspec(poolw.shape), lspec(ps.shape),
                      _const_spec(pool_win.shape)],
            out_specs=[tok(d_model), tok(d_sb), tok(d_sb), tok(d_sb), tok(d_mix)],
            out_shape=[jax.ShapeDtypeStruct((bsz, seq, d_model), F32),
                       jax.ShapeDtypeStruct((bsz, seq, d_sb), BF16),
                       jax.ShapeDtypeStruct((bsz, seq, d_sb), BF16),
                       jax.ShapeDtypeStruct((bsz, seq, d_sb), BF16),
                       jax.ShapeDtypeStruct((bsz, seq, d_mix), BF16)],
            scratch_shapes=[pltpu.VMEM((CONV_TAIL, d_lru), F32), pltpu.VMEM((POOL_TAIL, d_pool), F32),
                            pltpu.VMEM((1, d_lru), F32)],
            compiler_params=seq_params,
            name=f"mix_in_{layer}",
        )(x, g1, f1w1, f1w3, f1w2, gm, win, qg, kg, head_mean, convw, cb, wgate, bgate, lam, poolw, ps,
          pool_win)

        osb = pl.pallas_call(
            functools.partial(_attn_kernel, layer=layer),
            grid=(bsz, d_sb // LANES, seq // tq),
            in_specs=[pl.BlockSpec((1, tq, LANES), lambda b, hp, i: (b, i, hp)),
                      pl.BlockSpec((1, seq, LANES), lambda b, hp, i: (b, 0, hp)),
                      pl.BlockSpec((1, seq, LANES), lambda b, hp, i: (b, 0, hp)),
                      pl.BlockSpec((tk, tk), lambda b, hp, i: (0, 0)),
                      pl.BlockSpec(memory_space=pltpu.SMEM)],
            out_specs=pl.BlockSpec((1, tq, LANES), lambda b, hp, i: (b, i, hp)),
            out_shape=jax.ShapeDtypeStruct((bsz, seq, d_sb), BF16),
            scratch_shapes=[pltpu.VMEM((attn_rows, LANES), F32), pltpu.VMEM((attn_rows, 1), F32),
                            pltpu.VMEM((attn_rows, tk), BF16), pltpu.VMEM((attn_rows, tk), BF16)]
                           + [pltpu.VMEM((attn_rows, tk), F32)] * ATTN_Z_SLOTS,
            compiler_params=pltpu.CompilerParams(
                dimension_semantics=("arbitrary", "arbitrary", "arbitrary"), vmem_limit_bytes=VMEM_LIMIT),
            name=f"attn_{layer}",
        )(q, k, v, tri, attn_limit)

        x = pl.pallas_call(
            _mix_out_kernel,
            grid=tok_grid,
            in_specs=[tok(d_model), tok(d_sb), tok(d_mix),
                      pl.BlockSpec((None, 1, tm, ple_dim), lambda b, s, layer=layer: (layer, b, s, 0)),
                      lspec(wo.shape), lspec(g2.shape), lspec(f2w1.shape), lspec(f2w3.shape),
                      lspec(f2w2.shape), lspec(gp.shape), lspec(wg.shape), lspec(wp.shape)],
            out_specs=tok(d_model),
            out_shape=jax.ShapeDtypeStruct((bsz, seq, d_model), F32),
            compiler_params=seq_params,
            name=f"mix_out_{layer}",
        )(x1, osb, mix, p, wo, g2, f2w1, f2w3, f2w2, gp, wg, wp)
    return x
```

```python
import functools

import jax
import jax.numpy as jnp
from jax import lax
from jax.experimental import pallas as pl
from jax.experimental.pallas import tpu as pltpu

EPS = 1e-6
HEAD_DIM = 64
POOL_WINDOWS = (2, 4, 8, 16)
CONV_WIDTH = 4
LRU_C = 8.0

LANES = 128
MXU_TILE = 256
TOKEN_TILE = 512
ATTN_Q_TILE = 512
ATTN_Z_SLOTS = 6
FIRST_LEG = 4
LOG2E = 1.4426950408889634
UNDERFLOW_BITS = 1088.0
SOFTPLUS_CLAMP = 115.0
FFN_CHUNK = 6 * MXU_TILE
CONV_TAIL = 8
POOL_TAIL = 16
VMEM_LIMIT = 60 * 1024 * 1024

F32 = jnp.float32
BF16 = jnp.bfloat16


def _dot(a, b):
    return jnp.dot(a, b, preferred_element_type=F32)


def _rms(x, g):
    ms = jnp.mean(x * x, axis=-1, keepdims=True)
    return x * lax.rsqrt(ms + EPS) * g


def _ffn_half_step(x, g_ref, w1_ref, w3_ref, w2_ref):
    h = _rms(x, g_ref[...]).astype(BF16)
    d_ff = w1_ref.shape[1]
    acc = None
    for c0 in range(0, d_ff, FFN_CHUNK):
        c1 = min(c0 + FFN_CHUNK, d_ff)
        a = _dot(h, w1_ref[:, c0:c1])
        b = _dot(h, w3_ref[:, c0:c1])
        m = (a * jax.nn.sigmoid(a) * b).astype(BF16)
        part = _dot(m, w2_ref[c0:c1, :])
        acc = part if acc is None else acc + part
    return x + 0.5 * acc


def _head_rms(q, head_mean_ref, g):
    sq = (q * q).astype(BF16)
    width = q.shape[1]
    parts = []
    for c0 in range(0, width, MXU_TILE):
        c1 = min(c0 + MXU_TILE, width)
        parts.append(_dot(sq[:, c0:c1], head_mean_ref[c0:c1, c0:c1]))
    ms = jnp.concatenate(parts, axis=1)
    return q * lax.rsqrt(ms + EPS) * g


def _shift_rows(ext, shift, tail):
    if shift == 0:
        return ext[tail:]
    return pltpu.roll(ext, shift, axis=0)[tail:]


def _neg_expm1(y):
    poly = 1.0 / 40320.0
    for k in (5040.0, 720.0, 120.0, 24.0, 6.0, 2.0, 1.0):
        poly = poly * y + 1.0 / k
    series = -y * poly
    return jnp.where(y > -0.25, series, 1.0 - jnp.exp(y))


def _linear_scan(a, u, h0):
    rows = a.shape[0]
    row = lax.broadcasted_iota(jnp.int32, a.shape, 0)
    d = 1
    while d < rows:
        keep = row >= d
        a_prev = jnp.where(keep, pltpu.roll(a, d, axis=0), 1.0)
        u_prev = jnp.where(keep, pltpu.roll(u, d, axis=0), 0.0)
        u = a * u_prev + u
        a = a * a_prev
        d *= 2
    return u + a * h0


def _mix_in_kernel(x_ref, g1_ref, w1_ref, w3_ref, w2_ref, gm_ref, win_ref, qg_ref, kg_ref, hmean_ref,
                   convw_ref, convb_ref, wgate_ref, bgate_ref, lam_ref, poolw_ref, pscale_ref, pwin_ref,
                   x1_ref, q_ref, k_ref, v_ref, mix_ref,
                   xb_tail, xc_tail, h_carry, *, d_sb, d_lru, d_pool):
    s = pl.program_id(1)
    tm = x_ref.shape[1]

    @pl.when(s == 0)
    def _():
        xb_tail[...] = jnp.zeros_like(xb_tail)
        xc_tail[...] = jnp.zeros_like(xc_tail)
        h_carry[...] = jnp.zeros_like(h_carry)

    x1 = _ffn_half_step(x_ref[0], g1_ref, w1_ref, w3_ref, w2_ref)
    x1_ref[0] = x1

    h = _rms(x1, gm_ref[...]).astype(BF16)
    proj = _dot(h, win_ref[...])
    o = 0
    q = proj[:, o:o + d_sb]; o += d_sb
    k = proj[:, o:o + d_sb]; o += d_sb
    v = proj[:, o:o + d_sb]; o += d_sb
    xb = proj[:, o:o + d_lru]; o += d_lru
    gb = proj[:, o:o + d_lru]; o += d_lru
    xc = proj[:, o:o + d_pool]

    q_ref[0] = (_head_rms(q, hmean_ref, qg_ref[...]) * (HEAD_DIM ** -0.5 * LOG2E)).astype(BF16)
    k_ref[0] = _head_rms(k, hmean_ref, kg_ref[...]).astype(BF16)
    v_ref[0] = v.astype(BF16)

    ext = jnp.concatenate([xb_tail[...], xb], axis=0)
    conv = convb_ref[...] + convw_ref[CONV_WIDTH - 1:CONV_WIDTH, :] * xb
    for j in range(1, CONV_WIDTH):
        conv = conv + convw_ref[CONV_WIDTH - 1 - j:CONV_WIDTH - j, :] * _shift_rows(ext, j, CONV_TAIL)
    xb_tail[...] = xb[tm - CONV_TAIL:, :]
    gates = jax.nn.sigmoid(_dot(conv.astype(BF16), wgate_ref[...]) + bgate_ref[...])
    r_gate = gates[:, :d_lru]
    i_gate = gates[:, d_lru:]
    lam = lam_ref[...]
    softplus_neg_lam = jnp.maximum(-lam, 0.0) + jnp.log1p(jnp.exp(-jnp.abs(lam)))
    log_a = (-LRU_C * r_gate) * softplus_neg_lam
    a = jnp.exp(log_a)
    u = jnp.sqrt(_neg_expm1(2.0 * log_a)) * (i_gate * conv)
    hseq = _linear_scan(a, u, h_carry[...])
    h_carry[...] = hseq[tm - 1:tm, :]
    mix_ref[0, :, 0:d_lru] = (jax.nn.gelu(gb) * hseq).astype(BF16)

    ext = jnp.concatenate([xc_tail[...], xc], axis=0)
    lane = lax.broadcasted_iota(jnp.int32, ext.shape, 1)
    group = d_pool // len(POOL_WINDOWS)
    run = ext
    wsum = None
    for gi, w in enumerate(POOL_WINDOWS):
        run = run + pltpu.roll(run, w // 2, axis=0)
        wsum = run if wsum is None else jnp.where(lane >= gi * group, run, wsum)
    wsum = wsum[POOL_TAIL:]
    xc_tail[...] = xc[tm - POOL_TAIL:, :]
    pos = (s * tm + 1 + lax.broadcasted_iota(jnp.int32, xc.shape, 0)).astype(F32)
    count = jnp.minimum(pos, pwin_ref[...])
    pooled = wsum / count - xc
    mix_ref[0, :, d_lru:d_lru + d_pool] = (_dot(pooled.astype(BF16), poolw_ref[...]) * pscale_ref[...]).astype(BF16)


def _attn_kernel(q_ref, k_ref, v_ref, tri_ref, limit_ref, o_ref, acc_ref, carry_ref, wa_ref, wb_ref, *z_refs,
                 layer):
    i = pl.program_id(2)
    tq = q_ref.shape[1]
    tk = tri_ref.shape[1]
    heads = LANES // HEAD_DIM
    rows = heads * tq
    half = rows // 2
    lane = lax.broadcasted_iota(jnp.int32, (tk, LANES), 1)
    qs = jnp.concatenate(
        [jnp.where((lane >= hd * HEAD_DIM) & (lane < (hd + 1) * HEAD_DIM), q_ref[0, part * tk:(part + 1) * tk, :],
                   jnp.zeros((tk, LANES), q_ref.dtype))
         for part in range(tq // tk) for hd in range(heads)], axis=0)
    tri = tri_ref[...]
    acc_ref[...] = jnp.zeros_like(acc_ref)
    carry_ref[...] = jnp.zeros_like(carry_ref)

    def keys(ref, j):
        return ref[0, pl.ds(pl.multiple_of(j * tk, tk), tk), :]

    def scores(j, z_ref, lo=0):
        z_ref[lo:, :] = lax.dot_general(qs[lo:], keys(k_ref, j), (((1,), (1,)), ((), ())),
                                        preferred_element_type=F32)

    def weights(z_ref, w_ref, diagonal_offset=None, lo=0):
        z = z_ref[lo:, :]
        sp = jnp.maximum(z, jnp.log(1.0 + jnp.exp2(jnp.minimum(z, SOFTPLUS_CLAMP))) * LOG2E)
        if diagonal_offset is not None:
            r = lo + lax.broadcasted_iota(jnp.int32, (rows - lo, tk), 0)
            qpos = jnp.where(r >= half, tk, 0) + lax.rem(r, tk)
            kpos = diagonal_offset + lax.broadcasted_iota(jnp.int32, (rows - lo, tk), 1)
            strictly_before = kpos < qpos
            sp = jnp.where(strictly_before, sp, 0.0)
        c = _dot(sp.astype(BF16), tri)
        carry = carry_ref[lo:, :]
        w = jnp.exp2(z_ref[lo:, :] - c - carry)
        if diagonal_offset is not None:
            w = jnp.where(strictly_before, w, 0.0)
        w_ref[lo:, :] = w.astype(BF16)
        carry_ref[lo:, :] = carry + c[:, 0:1]

    def accumulate(j, w_ref, lo=0):
        acc_ref[lo:, :] += _dot(w_ref[lo:, :], keys(v_ref, j))

    top = 2 * i + 1
    w_refs = (wa_ref, wb_ref)
    n_z = len(z_refs)

    def diagonal_blocks():
        scores(top, z_refs[2], lo=half)
        scores(top - 1, z_refs[3])
        weights(z_refs[2], wa_ref, diagonal_offset=tk, lo=half)
        scores(jnp.maximum(top - 2, 0), z_refs[0])
        accumulate(top, wa_ref, lo=half)
        weights(z_refs[3], wb_ref, diagonal_offset=0)
        scores(jnp.maximum(top - 3, 0), z_refs[1])

    def run(ja, count, feed_next, base=0):
        for s in range(count):
            accumulate(ja - s + 1, w_refs[(s + 1) % 2])
            if feed_next or s + 2 < count:
                scores(jnp.maximum(ja - s - 2, 0), z_refs[(base + s + 2) % n_z])
            weights(z_refs[(base + s) % n_z], w_refs[s % 2])

    rounds = (2 * i) // n_z
    limit = limit_ref[layer]

    def unfinished():
        return jnp.min(carry_ref[...]) < limit

    def first_leg(u):
        run(top - 2 - n_z * u, FIRST_LEG, True)

    @pl.when(rounds == 0)
    def _():
        diagonal_blocks()

    @pl.when(rounds > 0)
    def _():
        diagonal_blocks()
        first_leg(0)

    def more(state):
        u, live, _ = state
        return jnp.logical_and(u < rounds, live > 0)

    def body(state):
        u, _, visited = state
        go_on = unfinished()

        @pl.when(go_on)
        def _():
            run(top - 2 - n_z * u - FIRST_LEG, n_z - FIRST_LEG, True, base=FIRST_LEG)

        live = jnp.logical_and(go_on, unfinished())

        @pl.when(jnp.logical_and(live, u + 1 < rounds))
        def _():
            first_leg(u + 1)

        return u + 1, live.astype(jnp.int32), visited + jnp.where(go_on, n_z, FIRST_LEG)

    _, live, visited = lax.while_loop(more, body, (jnp.int32(0), jnp.int32(1), jnp.int32(0)))
    left = jnp.where(live > 0, 2 * i - n_z * rounds, 0)
    for count in range(2, n_z, 2):
        @pl.when(left == count)
        def _(count=count):
            run(top - 2 - n_z * rounds, count, False)

    accumulate(top - 1 - visited - left, wb_ref)
    for part in range(tq // tk):
        base = part * half
        out = acc_ref[base:base + tk, :]
        for hd in range(1, heads):
            out = jnp.where(lane >= hd * HEAD_DIM, acc_ref[base + hd * tk:base + (hd + 1) * tk, :], out)
        o_ref[0, part * tk:(part + 1) * tk, :] = out.astype(o_ref.dtype)


def _mix_out_kernel(x1_ref, osb_ref, mix_ref, p_ref, wo_ref, g2_ref, w1_ref, w3_ref, w2_ref,
                    gp_ref, wg_ref, wp_ref, out_ref):
    mixed = jnp.concatenate([osb_ref[0], mix_ref[0]], axis=1)
    x = x1_ref[0] + _dot(mixed, wo_ref[...])
    x = _ffn_half_step(x, g2_ref, w1_ref, w3_ref, w2_ref)
    gate = jax.nn.sigmoid(_dot(_rms(x, gp_ref[...]).astype(BF16), wg_ref[...]))
    emb = _dot(p_ref[0].astype(BF16), wp_ref[...])
    out_ref[0] = x + gate * emb


def _block_diag(w):
    l, h, i, j = w.shape
    eye = jnp.eye(h, dtype=w.dtype)
    return jnp.einsum('lhij,hg->lhigj', w, eye).reshape(l, h * i, h * j)


def _layer_spec(shape, layer):
    nd = len(shape) - 1
    return pl.BlockSpec((None,) + tuple(shape[1:]), lambda b, s: (layer,) + (0,) * nd,
                        pipeline_mode=pl.Buffered(1))


def _const_spec(shape):
    nd = len(shape)
    return pl.BlockSpec(tuple(shape), lambda *_: (0,) * nd, pipeline_mode=pl.Buffered(1))


def kernel(x, p, ffn1_norm, ffn1_w1, ffn1_w3, ffn1_w2, mix_norm, w_in, q_norm, k_norm, conv_w, conv_b,
           lru_wa, lru_ba, lru_wx, lru_bx, lru_lambda, pool_w, pool_scale, w_out, ffn2_norm, ffn2_w1,
           ffn2_w3, ffn2_w2, ple_norm, ple_gate_w, ple_proj):
    bsz, seq, d_model = x.shape
    depth = p.shape[0]
    ple_dim = p.shape[-1]
    d_lru = conv_b.shape[-1]
    d_pool = pool_scale.shape[-1]
    d_sb = (w_in.shape[-1] - 2 * d_lru - d_pool) // 3
    n_heads = d_sb // HEAD_DIM
    d_mix = d_lru + d_pool
    tm = min(TOKEN_TILE, seq)
    tq = min(ATTN_Q_TILE, seq)
    tk = tq // 2
    assert seq % tm == 0 and seq % tq == 0 and tk % 8 == 0 and d_sb % LANES == 0

    row = lambda a: a.reshape(depth, 1, a.shape[-1]).astype(F32)
    bf = lambda a: a.astype(BF16)
    g1, gm, g2, gp = row(ffn1_norm), row(mix_norm), row(ffn2_norm), row(ple_norm)
    qg = row(jnp.tile(q_norm, (1, n_heads)))
    kg = row(jnp.tile(k_norm, (1, n_heads)))
    cb, lam, ps = row(conv_b), row(lru_lambda), row(pool_scale)
    bgate = row(jnp.concatenate([lru_ba, lru_bx], axis=-1))
    wgate = bf(jnp.concatenate([_block_diag(lru_wa), _block_diag(lru_wx)], axis=-1))
    poolw = bf(_block_diag(pool_w))
    f1w1, f1w3, f1w2 = bf(ffn1_w1), bf(ffn1_w3), bf(ffn1_w2)
    f2w1, f2w3, f2w2 = bf(ffn2_w1), bf(ffn2_w3), bf(ffn2_w2)
    win, wo, wg, wp = bf(w_in), bf(w_out), bf(ple_gate_w), bf(ple_proj)
    convw = conv_w.astype(F32)
    head_mean = bf(jnp.kron(jnp.eye(n_heads, dtype=F32), jnp.full((HEAD_DIM, HEAD_DIM), 1.0 / HEAD_DIM, F32)))
    pool_win = jnp.repeat(jnp.asarray(POOL_WINDOWS, F32), d_pool // len(POOL_WINDOWS)).reshape(1, d_pool)
    idx = jnp.arange(tk)
    tri = (idx[:, None] >= idx[None, :]).astype(BF16)
    attn_rows = (LANES // HEAD_DIM) * tq
    score_bound = (HEAD_DIM ** 0.5 * LOG2E) * jnp.max(jnp.abs(q_norm), axis=-1) * jnp.max(jnp.abs(k_norm), axis=-1)
    attn_limit = (UNDERFLOW_BITS + score_bound * 2.0 ** -8).astype(F32)

    tok_grid = (bsz, seq // tm)
    tok = lambda width: pl.BlockSpec((1, tm, width), lambda b, s: (b, s, 0))
    seq_params = pltpu.CompilerParams(dimension_semantics=("arbitrary", "arbitrary"),
                                      vmem_limit_bytes=VMEM_LIMIT)

    for layer in range(depth):
        lspec = functools.partial(_layer_spec, layer=layer)
        x1, q, k, v, mix = pl.pallas_call(
            functools.partial(_mix_in_kernel, d_sb=d_sb, d_lru=d_lru, d_pool=d_pool),
            grid=tok_grid,
            in_specs=[tok(d_model), lspec(g1.shape), lspec(f1w1.shape), lspec(f1w3.shape), lspec(f1w2.shape),
                      lspec(gm.shape), lspec(win.shape), lspec(qg.shape), lspec(kg.shape),
                      _const_spec(head_mean.shape), lspec(convw.shape), lspec(cb.shape), lspec(wgate.shape),
                      lspec(bgate.shape), lspec(lam.shape), lspec(poolw.shape), lspec(ps.shape),
                      _const_spec(pool_win.shape)],
            out_specs=[tok(d_model), tok(d_sb), tok(d_sb), tok(d_sb), tok(d_mix)],
            out_shape=[jax.ShapeDtypeStruct((bsz, seq, d_model), F32),
                       jax.ShapeDtypeStruct((bsz, seq, d_sb), BF16),
                       jax.ShapeDtypeStruct((bsz, seq, d_sb), BF16),
                       jax.ShapeDtypeStruct((bsz, seq, d_sb), BF16),
                       jax.ShapeDtypeStruct((bsz, seq, d_mix), BF16)],
            scratch_shapes=[pltpu.VMEM((CONV_TAIL, d_lru), F32), pltpu.VMEM((POOL_TAIL, d_pool), F32),
                            pltpu.VMEM((1, d_lru), F32)],
            compiler_params=seq_params,
            name=f"mix_in_{layer}",
        )(x, g1, f1w1, f1w3, f1w2, gm, win, qg, kg, head_mean, convw, cb, wgate, bgate, lam, poolw, ps,
          pool_win)

        osb = pl.pallas_call(
            functools.partial(_attn_kernel, layer=layer),
            grid=(bsz, d_sb // LANES, seq // tq),
            in_specs=[pl.BlockSpec((1, tq, LANES), lambda b, hp, i: (b, i, hp)),
                      pl.BlockSpec((1, seq, LANES), lambda b, hp, i: (b, 0, hp)),
                      pl.BlockSpec((1, seq, LANES), lambda b, hp, i: (b, 0, hp)),
                      pl.BlockSpec((tk, tk), lambda b, hp, i: (0, 0)),
                      pl.BlockSpec(memory_space=pltpu.SMEM)],
            out_specs=pl.BlockSpec((1, tq, LANES), lambda b, hp, i: (b, i, hp)),
            out_shape=jax.ShapeDtypeStruct((bsz, seq, d_sb), BF16),
            scratch_shapes=[pltpu.VMEM((attn_rows, LANES), F32), pltpu.VMEM((attn_rows, 1), F32),
                            pltpu.VMEM((attn_rows, tk), BF16), pltpu.VMEM((attn_rows, tk), BF16)]
                           + [pltpu.VMEM((attn_rows, tk), F32)] * ATTN_Z_SLOTS,
            compiler_params=pltpu.CompilerParams(
                dimension_semantics=("arbitrary", "arbitrary", "arbitrary"), vmem_limit_bytes=VMEM_LIMIT),
            name=f"attn_{layer}",
        )(q, k, v, tri, attn_limit)

        x = pl.pallas_call(
            _mix_out_kernel,
            grid=tok_grid,
            in_specs=[tok(d_model), tok(d_sb), tok(d_mix),
                      pl.BlockSpec((None, 1, tm, ple_dim), lambda b, s, layer=layer: (layer, b, s, 0)),
                      lspec(wo.shape), lspec(g2.shape), lspec(f2w1.shape), lspec(f2w3.shape),
                      lspec(f2w2.shape), lspec(gp.shape), lspec(wg.shape), lspec(wp.shape)],
            out_specs=tok(d_model),
            out_shape=jax.ShapeDtypeStruct((bsz, seq, d_model), F32),
            compiler_params=seq_params,
            name=f"mix_out_{layer}",
        )(x1, osb, mix, p, wo, g2, f2w1, f2w3, f2w2, gp, wg, wp)
    return x
```

```python
import functools

import jax
import jax.numpy as jnp
from jax import lax
from jax.experimental import pallas as pl
from jax.experimental.pallas import tpu as pltpu

EPS = 1e-6
HEAD_DIM = 64
POOL_WINDOWS = (2, 4, 8, 16)
CONV_WIDTH = 4
LRU_C = 8.0

LANES = 128
MXU_TILE = 256
TOKEN_TILE = 512
ATTN_Q_TILE = 512
ATTN_Z_SLOTS = 6
FIRST_LEG = 4
LOG2E = 1.4426950408889634
UNDERFLOW_BITS = 1088.0
SOFTPLUS_CLAMP = 115.0
FFN_CHUNK = 6 * MXU_TILE
CONV_TAIL = 8
POOL_TAIL = 16
VMEM_LIMIT = 60 * 1024 * 1024

F32 = jnp.float32
BF16 = jnp.bfloat16


def _dot(a, b):
    return jnp.dot(a, b, preferred_element_type=F32)


def _rms(x, g):
    ms = jnp.mean(x * x, axis=-1, keepdims=True)
    return x * lax.rsqrt(ms + EPS) * g


def _ffn_half_step(x, g_ref, w1_ref, w3_ref, w2_ref):
    h = _rms(x, g_ref[...]).astype(BF16)
    d_ff = w1_ref.shape[1]
    acc = None
    for c0 in range(0, d_ff, FFN_CHUNK):
        c1 = min(c0 + FFN_CHUNK, d_ff)
        a = _dot(h, w1_ref[:, c0:c1])
        b = _dot(h, w3_ref[:, c0:c1])
        m = (a * jax.nn.sigmoid(a) * b).astype(BF16)
        part = _dot(m, w2_ref[c0:c1, :])
        acc = part if acc is None else acc + part
    return x + 0.5 * acc


def _head_rms(q, head_mean_ref, g):
    sq = (q * q).astype(BF16)
    width = q.shape[1]
    parts = []
    for c0 in range(0, width, MXU_TILE):
        c1 = min(c0 + MXU_TILE, width)
        parts.append(_dot(sq[:, c0:c1], head_mean_ref[c0:c1, c0:c1]))
    ms = jnp.concatenate(parts, axis=1)
    return q * lax.rsqrt(ms + EPS) * g


def _shift_rows(ext, shift, tail):
    if shift == 0:
        return ext[tail:]
    return pltpu.roll(ext, shift, axis=0)[tail:]


def _neg_expm1(y):
    poly = 1.0 / 40320.0
    for k in (5040.0, 720.0, 120.0, 24.0, 6.0, 2.0, 1.0):
        poly = poly * y + 1.0 / k
    series = -y * poly
    return jnp.where(y > -0.25, series, 1.0 - jnp.exp(y))


def _linear_scan(a, u, h0):
    rows = a.shape[0]
    row = lax.broadcasted_iota(jnp.int32, a.shape, 0)
    d = 1
    while d < rows:
        keep = row >= d
        a_prev = jnp.where(keep, pltpu.roll(a, d, axis=0), 1.0)
        u_prev = jnp.where(keep, pltpu.roll(u, d, axis=0), 0.0)
        u = a * u_prev + u
        a = a * a_prev
        d *= 2
    return u + a * h0


def _mix_in_kernel(x_ref, g1_ref, w1_ref, w3_ref, w2_ref, gm_ref, win_ref, qg_ref, kg_ref, hmean_ref,
                   convw_ref, convb_ref, wgate_ref, bgate_ref, lam_ref, poolw_ref, pscale_ref, pwin_ref,
                   x1_ref, q_ref, k_ref, v_ref, mix_ref,
                   xb_tail, xc_tail, h_carry, *, d_sb, d_lru, d_pool):
    s = pl.program_id(1)
    tm = x_ref.shape[1]

    @pl.when(s == 0)
    def _():
        xb_tail[...] = jnp.zeros_like(xb_tail)
        xc_tail[...] = jnp.zeros_like(xc_tail)
        h_carry[...] = jnp.zeros_like(h_carry)

    x1 = _ffn_half_step(x_ref[0], g1_ref, w1_ref, w3_ref, w2_ref)
    x1_ref[0] = x1

    h = _rms(x1, gm_ref[...]).astype(BF16)
    proj = _dot(h, win_ref[...])
    o = 0
    q = proj[:, o:o + d_sb]; o += d_sb
    k = proj[:, o:o + d_sb]; o += d_sb
    v = proj[:, o:o + d_sb]; o += d_sb
    xb = proj[:, o:o + d_lru]; o += d_lru
    gb = proj[:, o:o + d_lru]; o += d_lru
    xc = proj[:, o:o + d_pool]

    q_ref[0] = (_head_rms(q, hmean_ref, qg_ref[...]) * (HEAD_DIM ** -0.5 * LOG2E)).astype(BF16)
    k_ref[0] = _head_rms(k, hmean_ref, kg_ref[...]).astype(BF16)
    v_ref[0] = v.astype(BF16)

    ext = jnp.concatenate([xb_tail[...], xb], axis=0)
    conv = convb_ref[...] + convw_ref[CONV_WIDTH - 1:CONV_WIDTH, :] * xb
    for j in range(1, CONV_WIDTH):
        conv = conv + convw_ref[CONV_WIDTH - 1 - j:CONV_WIDTH - j, :] * _shift_rows(ext, j, CONV_TAIL)
    xb_tail[...] = xb[tm - CONV_TAIL:, :]
    gates = jax.nn.sigmoid(_dot(conv.astype(BF16), wgate_ref[...]) + bgate_ref[...])
    r_gate = gates[:, :d_lru]
    i_gate = gates[:, d_lru:]
    lam = lam_ref[...]
    softplus_neg_lam = jnp.maximum(-lam, 0.0) + jnp.log1p(jnp.exp(-jnp.abs(lam)))
    log_a = (-LRU_C * r_gate) * softplus_neg_lam
    a = jnp.exp(log_a)
    u = jnp.sqrt(_neg_expm1(2.0 * log_a)) * (i_gate * conv)
    hseq = _linear_scan(a, u, h_carry[...])
    h_carry[...] = hseq[tm - 1:tm, :]
    mix_ref[0, :, 0:d_lru] = (jax.nn.gelu(gb) * hseq).astype(BF16)

    ext = jnp.concatenate([xc_tail[...], xc], axis=0)
    lane = lax.broadcasted_iota(jnp.int32, ext.shape, 1)
    group = d_pool // len(POOL_WINDOWS)
    run = ext
    wsum = None
    for gi, w in enumerate(POOL_WINDOWS):
        run = run + pltpu.roll(run, w // 2, axis=0)
        wsum = run if wsum is None else jnp.where(lane >= gi * group, run, wsum)
    wsum = wsum[POOL_TAIL:]
    xc_tail[...] = xc[tm - POOL_TAIL:, :]
    pos = (s * tm + 1 + lax.broadcasted_iota(jnp.int32, xc.shape, 0)).astype(F32)
    count = jnp.minimum(pos, pwin_ref[...])
    pooled = wsum / count - xc
    mix_ref[0, :, d_lru:d_lru + d_pool] = (_dot(pooled.astype(BF16), poolw_ref[...]) * pscale_ref[...]).astype(BF16)


def _attn_kernel(q_ref, k_ref, v_ref, tri_ref, limit_ref, o_ref, acc_ref, carry_ref, wa_ref, wb_ref, *z_refs,
                 layer):
    i = pl.program_id(2)
    tq = q_ref.shape[1]
    tk = tri_ref.shape[1]
    heads = LANES // HEAD_DIM
    rows = heads * tq
    half = rows // 2
    lane = lax.broadcasted_iota(jnp.int32, (tk, LANES), 1)
    qs = jnp.concatenate(
        [jnp.where((lane >= hd * HEAD_DIM) & (lane < (hd + 1) * HEAD_DIM), q_ref[0, part * tk:(part + 1) * tk, :],
                   jnp.zeros((tk, LANES), q_ref.dtype))
         for part in range(tq // tk) for hd in range(heads)], axis=0)
    tri = tri_ref[...]
    acc_ref[...] = jnp.zeros_like(acc_ref)
    carry_ref[...] = jnp.zeros_like(carry_ref)

    def keys(ref, j):
        return ref[0, pl.ds(pl.multiple_of(j * tk, tk), tk), :]

    def scores(j, z_ref, lo=0):
        z_ref[lo:, :] = lax.dot_general(qs[lo:], keys(k_ref, j), (((1,), (1,)), ((), ())),
                                        preferred_element_type=F32)

    def weights(z_ref, w_ref, diagonal_offset=None, lo=0):
        for r0 in range(lo, rows, tk):
            z = z_ref[r0:r0 + tk, :]
            sp = jnp.maximum(z, jnp.log(1.0 + jnp.exp2(jnp.minimum(z, SOFTPLUS_CLAMP))) * LOG2E)
            if diagonal_offset is not None:
                qpos = (tk if r0 >= half else 0) + lax.broadcasted_iota(jnp.int32, (tk, tk), 0)
                kpos = diagonal_offset + lax.broadcasted_iota(jnp.int32, (tk, tk), 1)
                strictly_before = kpos < qpos
                sp = jnp.where(strictly_before, sp, 0.0)
            c = _dot(sp.astype(BF16), tri)
            carry = carry_ref[r0:r0 + tk, :]
            w = jnp.exp2(z_ref[r0:r0 + tk, :] - c - carry)
            if diagonal_offset is not None:
                w = jnp.where(strictly_before, w, 0.0)
            w_ref[r0:r0 + tk, :] = w.astype(BF16)
            carry_ref[r0:r0 + tk, :] = carry + c[:, 0:1]

    def accumulate(j, w_ref, lo=0):
        acc_ref[lo:, :] += _dot(w_ref[lo:, :], keys(v_ref, j))

    top = 2 * i + 1
    w_refs = (wa_ref, wb_ref)
    n_z = len(z_refs)
    scores(top, z_refs[2], lo=half)
    scores(top - 1, z_refs[3])
    weights(z_refs[2], wa_ref, diagonal_offset=tk, lo=half)
    scores(jnp.maximum(top - 2, 0), z_refs[0])
    accumulate(top, wa_ref, lo=half)
    weights(z_refs[3], wb_ref, diagonal_offset=0)
    scores(jnp.maximum(top - 3, 0), z_refs[1])

    def run(ja, count, feed_next, base=0):
        for s in range(count):
            accumulate(ja - s + 1, w_refs[(s + 1) % 2])
            if feed_next or s + 2 < count:
                scores(jnp.maximum(ja - s - 2, 0), z_refs[(base + s + 2) % n_z])
            weights(z_refs[(base + s) % n_z], w_refs[s % 2])

    rounds = (2 * i) // n_z
    limit = limit_ref[layer]

    def unfinished():
        return jnp.min(carry_ref[...]) < limit

    def more(state):
        u, live, _ = state
        return jnp.logical_and(u < rounds, live > 0)

    def body(state):
        u, _, visited = state
        ja = top - 2 - n_z * u
        run(ja, FIRST_LEG, True)
        go_on = unfinished()

        @pl.when(go_on)
        def _():
            run(ja - FIRST_LEG, n_z - FIRST_LEG, True, base=FIRST_LEG)

        live = jnp.logical_and(go_on, unfinished()).astype(jnp.int32)
        return u + 1, live, visited + jnp.where(go_on, n_z, FIRST_LEG)

    _, live, visited = lax.while_loop(more, body, (jnp.int32(0), jnp.int32(1), jnp.int32(0)))
    left = jnp.where(live > 0, 2 * i - n_z * rounds, 0)
    for count in range(2, n_z, 2):
        @pl.when(left == count)
        def _(count=count):
            run(top - 2 - n_z * rounds, count, False)

    accumulate(top - 1 - visited - left, wb_ref)
    for part in range(tq // tk):
        base = part * half
        out = acc_ref[base:base + tk, :]
        for hd in range(1, heads):
            out = jnp.where(lane >= hd * HEAD_DIM, acc_ref[base + hd * tk:base + (hd + 1) * tk, :], out)
        o_ref[0, part * tk:(part + 1) * tk, :] = out.astype(o_ref.dtype)


def _mix_out_kernel(x1_ref, osb_ref, mix_ref, p_ref, wo_ref, g2_ref, w1_ref, w3_ref, w2_ref,
                    gp_ref, wg_ref, wp_ref, out_ref):
    mixed = jnp.concatenate([osb_ref[0], mix_ref[0]], axis=1)
    x = x1_ref[0] + _dot(mixed, wo_ref[...])
    x = _ffn_half_step(x, g2_ref, w1_ref, w3_ref, w2_ref)
    gate = jax.nn.sigmoid(_dot(_rms(x, gp_ref[...]).astype(BF16), wg_ref[...]))
    emb = _dot(p_ref[0].astype(BF16), wp_ref[...])
    out_ref[0] = x + gate * emb


def _block_diag(w):
    l, h, i, j = w.shape
    eye = jnp.eye(h, dtype=w.dtype)
    return jnp.einsum('lhij,hg->lhigj', w, eye).reshape(l, h * i, h * j)


def _layer_spec(shape, layer):
    nd = len(shape) - 1
    return pl.BlockSpec((None,) + tuple(shape[1:]), lambda b, s: (layer,) + (0,) * nd,
                        pipeline_mode=pl.Buffered(1))


def _const_spec(shape):
    nd = len(shape)
    return pl.BlockSpec(tuple(shape), lambda *_: (0,) * nd, pipeline_mode=pl.Buffered(1))


def kernel(x, p, ffn1_norm, ffn1_w1, ffn1_w3, ffn1_w2, mix_norm, w_in, q_norm, k_norm, conv_w, conv_b,
           lru_wa, lru_ba, lru_wx, lru_bx, lru_lambda, pool_w, pool_scale, w_out, ffn2_norm, ffn2_w1,
           ffn2_w3, ffn2_w2, ple_norm, ple_gate_w, ple_proj):
    bsz, seq, d_model = x.shape
    depth = p.shape[0]
    ple_dim = p.shape[-1]
    d_lru = conv_b.shape[-1]
    d_pool = pool_scale.shape[-1]
    d_sb = (w_in.shape[-1] - 2 * d_lru - d_pool) // 3
    n_heads = d_sb // HEAD_DIM
    d_mix = d_lru + d_pool
    tm = min(TOKEN_TILE, seq)
    tq = min(ATTN_Q_TILE, seq)
    tk = tq // 2
    assert seq % tm == 0 and seq % tq == 0 and tk % 8 == 0 and d_sb % LANES == 0

    row = lambda a: a.reshape(depth, 1, a.shape[-1]).astype(F32)
    bf = lambda a: a.astype(BF16)
    g1, gm, g2, gp = row(ffn1_norm), row(mix_norm), row(ffn2_norm), row(ple_norm)
    qg = row(jnp.tile(q_norm, (1, n_heads)))
    kg = row(jnp.tile(k_norm, (1, n_heads)))
    cb, lam, ps = row(conv_b), row(lru_lambda), row(pool_scale)
    bgate = row(jnp.concatenate([lru_ba, lru_bx], axis=-1))
    wgate = bf(jnp.concatenate([_block_diag(lru_wa), _block_diag(lru_wx)], axis=-1))
    poolw = bf(_block_diag(pool_w))
    f1w1, f1w3, f1w2 = bf(ffn1_w1), bf(ffn1_w3), bf(ffn1_w2)
    f2w1, f2w3, f2w2 = bf(ffn2_w1), bf(ffn2_w3), bf(ffn2_w2)
    win, wo, wg, wp = bf(w_in), bf(w_out), bf(ple_gate_w), bf(ple_proj)
    convw = conv_w.astype(F32)
    head_mean = bf(jnp.kron(jnp.eye(n_heads, dtype=F32), jnp.full((HEAD_DIM, HEAD_DIM), 1.0 / HEAD_DIM, F32)))
    pool_win = jnp.repeat(jnp.asarray(POOL_WINDOWS, F32), d_pool // len(POOL_WINDOWS)).reshape(1, d_pool)
    idx = jnp.arange(tk)
    tri = (idx[:, None] >= idx[None, :]).astype(BF16)
    attn_rows = (LANES // HEAD_DIM) * tq
    score_bound = (HEAD_DIM ** 0.5 * LOG2E) * jnp.max(jnp.abs(q_norm), axis=-1) * jnp.max(jnp.abs(k_norm), axis=-1)
    attn_limit = (UNDERFLOW_BITS + score_bound * 2.0 ** -8).astype(F32)

    tok_grid = (bsz, seq // tm)
    tok = lambda width: pl.BlockSpec((1, tm, width), lambda b, s: (b, s, 0))
    seq_params = pltpu.CompilerParams(dimension_semantics=("arbitrary", "arbitrary"),
                                      vmem_limit_bytes=VMEM_LIMIT)

    for layer in range(depth):
        lspec = functools.partial(_layer_spec, layer=layer)
        x1, q, k, v, mix = pl.pallas_call(
            functools.partial(_mix_in_kernel, d_sb=d_sb, d_lru=d_lru, d_pool=d_pool),
            grid=tok_grid,
            in_specs=[tok(d_model), lspec(g1.shape), lspec(f1w1.shape), lspec(f1w3.shape), lspec(f1w2.shape),
                      lspec(gm.shape), lspec(win.shape), lspec(qg.shape), lspec(kg.shape),
                      _const_spec(head_mean.shape), lspec(convw.shape), lspec(cb.shape), lspec(wgate.shape),
                      lspec(bgate.shape), lspec(lam.shape), lspec(poolw.shape), lspec(ps.shape),
                      _const_spec(pool_win.shape)],
            out_specs=[tok(d_model), tok(d_sb), tok(d_sb), tok(d_sb), tok(d_mix)],
            out_shape=[jax.ShapeDtypeStruct((bsz, seq, d_model), F32),
                       jax.ShapeDtypeStruct((bsz, seq, d_sb), BF16),
                       jax.ShapeDtypeStruct((bsz, seq, d_sb), BF16),
                       jax.ShapeDtypeStruct((bsz, seq, d_sb), BF16),
                       jax.ShapeDtypeStruct((bsz, seq, d_mix), BF16)],
            scratch_shapes=[pltpu.VMEM((CONV_TAIL, d_lru), F32), pltpu.VMEM((POOL_TAIL, d_pool), F32),
                            pltpu.VMEM((1, d_lru), F32)],
            compiler_params=seq_params,
            name=f"mix_in_{layer}",
        )(x, g1, f1w1, f1w3, f1w2, gm, win, qg, kg, head_mean, convw, cb, wgate, bgate, lam, poolw, ps,
          pool_win)

        osb = pl.pallas_call(
            functools.partial(_attn_kernel, layer=layer),
            grid=(bsz, d_sb // LANES, seq // tq),
            in_specs=[pl.BlockSpec((1, tq, LANES), lambda b, hp, i: (b, i, hp)),
                      pl.BlockSpec((1, seq, LANES), lambda b, hp, i: (b, 0, hp)),
                      pl.BlockSpec((1, seq, LANES), lambda b, hp, i: (b, 0, hp)),
                      pl.BlockSpec((tk, tk), lambda b, hp, i: (0, 0)),
                      pl.BlockSpec(memory_space=pltpu.SMEM)],
            out_specs=pl.BlockSpec((1, tq, LANES), lambda b, hp, i: (b, i, hp)),
            out_shape=jax.ShapeDtypeStruct((bsz, seq, d_sb), BF16),
            scratch_shapes=[pltpu.VMEM((attn_rows, LANES), F32), pltpu.VMEM((attn_rows, 1), F32),
                            pltpu.VMEM((attn_rows, tk), BF16), pltpu.VMEM((attn_rows, tk), BF16)]
                           + [pltpu.VMEM((attn_rows, tk), F32)] * ATTN_Z_SLOTS,
            compiler_params=pltpu.CompilerParams(
                dimension_semantics=("arbitrary", "arbitrary", "arbitrary"), vmem_limit_bytes=VMEM_LIMIT),
            name=f"attn_{layer}",
        )(q, k, v, tri, attn_limit)

        x = pl.pallas_call(
            _mix_out_kernel,
            grid=tok_grid,
            in_specs=[tok(d_model), tok(d_sb), tok(d_mix),
                      pl.BlockSpec((None, 1, tm, ple_dim), lambda b, s, layer=layer: (layer, b, s, 0)),
                      lspec(wo.shape), lspec(g2.shape), lspec(f2w1.shape), lspec(f2w3.shape),
                      lspec(f2w2.shape), lspec(gp.shape), lspec(wg.shape), lspec(wp.shape)],
            out_specs=tok(d_model),
            out_shape=jax.ShapeDtypeStruct((bsz, seq, d_model), F32),
            compiler_params=seq_params,
            name=f"mix_out_{layer}",
        )(x1, osb, mix, p, wo, g2, f2w1, f2w3, f2w2, gp, wg, wp)
    return x
```
